```python
import jax, jax.numpy as jnp
from jax import lax
import numpy as np

D_MODEL = 2048
BATCH = 8
SEQ = 2048
DEPTH = 1
DEC_BATCH = 32
DEC_SEQ = 8
PAST_LEN = 16384
PAGE_SIZE = 128

N_ATT_HEADS = 12
HEAD_DIM = 128
D_ATT = N_ATT_HEADS * HEAD_DIM
ATT_SCALE = HEAD_DIM ** -0.5
PATTERNS = ((128, 1), (512, 4), (2048, 16))
MAX_WINDOW = 2048
BLOCK = 128
WIN_BUF = min(MAX_WINDOW, PAST_LEN)
D_LRU = D_MODEL - D_ATT
N_LRU_BLOCKS = 8
LRU_BLOCK = D_LRU // N_LRU_BLOCKS
CONV_WIDTH = 4
LRU_C = 8.0
D_IN = 3 * D_ATT + 2 * D_LRU
SPLITS = (D_ATT, 2 * D_ATT, 3 * D_ATT, 3 * D_ATT + D_LRU)
N_KEYS = 128
N_EXPERTS = N_KEYS * N_KEYS
PEER_HEADS = 8
D_KEY = 256
HALF_KEY = D_KEY // 2
TOPK = 16
TOKEN_BLOCK = 128
ALPHA = (2.0 * DEPTH) ** 0.25
BETA = (8.0 * DEPTH) ** -0.25
LN_EPS = 1e-5

kernel_name = 'hymba_dilated_rglru_peer_step'


def layer_norm(x, g, b):
    xf = x.astype(jnp.float32)
    mu = xf.mean(-1, keepdims=True)
    var = jnp.mean(jnp.square(xf - mu), -1, keepdims=True)
    return ((xf - mu) * lax.rsqrt(var + LN_EPS) * g + b).astype(x.dtype)


def alibi_slopes():
    return 2.0 ** (-8.0 * jnp.arange(1, N_ATT_HEADS + 1, dtype=jnp.float32) / N_ATT_HEADS)


def dilated_prompt(q, k, v, window, dil, slopes):
    B, S, H, E = q.shape
    nb = window // dil
    L = S // dil
    L_pad = -(-L // BLOCK) * BLOCK
    nc = L_pad // BLOCK

    def to_classes(t):
        t = jnp.pad(t, ((0, 0), (0, dil * L_pad - S), (0, 0), (0, 0)))
        return t.reshape(B, L_pad, dil, H, E)

    def key_blocks(t):
        t = jnp.pad(to_classes(t), ((0, 0), (BLOCK, 0), (0, 0), (0, 0), (0, 0)))
        t = t.reshape(B, nc + 1, BLOCK, dil, H, E)
        return jnp.concatenate([t[:, :-1], t[:, 1:]], axis=2)

    qc = to_classes(q).reshape(B, nc, BLOCK, dil, H, E)
    kc = key_blocks(k)
    vc = key_blocks(v)
    s = jnp.einsum('bcqrhe,bckrhe->bcrhqk', qc, kc, preferred_element_type=jnp.float32) * ATT_SCALE
    q_loc = jnp.arange(BLOCK)[:, None]
    k_loc = jnp.arange(2 * BLOCK)[None, :]
    dist = q_loc + BLOCK - k_loc
    key_idx = jnp.arange(nc)[:, None, None] * BLOCK - BLOCK + k_loc[None]
    valid = (dist >= 0) & (dist <= nb) & (key_idx >= 0)
    bias = -slopes[:, None, None] * (dil * dist).astype(jnp.float32)
    s = jnp.where(valid[None, :, None, None], s + bias, -jnp.inf)
    m = s.max(-1)
    p = jnp.exp(s - m[..., None])
    l = p.sum(-1)
    o = jnp.einsum('bcrhqk,bckrhe->bcqrhe', p, vc)
    o = o.reshape(B, L_pad * dil, H, E)[:, :S]
    m = m.transpose(0, 1, 4, 2, 3).reshape(B, L_pad * dil, H)[:, :S]
    l = l.transpose(0, 1, 4, 2, 3).reshape(B, L_pad * dil, H)[:, :S]
    return m, l, o


def dilated_sample(q, k_cat, v_cat, window, dil, slopes):
    T = q.shape[1]
    n_past = k_cat.shape[1] - T
    nb = window // dil
    dist = jnp.arange(nb + 1)
    idx = n_past + jnp.arange(T)[:, None] - dil * dist[None, :]
    valid = idx >= 0
    idx = jnp.maximum(idx, 0)
    kg = k_cat[:, idx]
    vg = v_cat[:, idx]
    s = jnp.einsum('bthe,btkhe->bhtk', q, kg, preferred_element_type=jnp.float32) * ATT_SCALE
    bias = -slopes[:, None, None] * (dil * dist).astype(jnp.float32)[None, None, :]
    s = jnp.where(valid[None, None], s + bias[None], -jnp.inf)
    m = s.max(-1)
    p = jnp.exp(s - m[..., None])
    l = p.sum(-1)
    o = jnp.einsum('bhtk,btkhe->bthe', p, vg)
    return m.transpose(0, 2, 1), l.transpose(0, 2, 1), o


def merge_patterns(parts):
    m_all = jnp.stack([pt[0] for pt in parts])
    l_all = jnp.stack([pt[1] for pt in parts])
    o_all = jnp.stack([pt[2] for pt in parts])
    w = jnp.exp(m_all - m_all.max(0))
    num = jnp.einsum('pbth,pbthe->bthe', w, o_all)
    den = jnp.einsum('pbth,pbth->bth', w, l_all)
    return num / den[..., None]


def causal_conv(x_ext, w, b):
    T = x_ext.shape[1] - (CONV_WIDTH - 1)
    return sum(x_ext[:, j:j + T] * w[j] for j in range(CONV_WIDTH)) + b


def rg_lru(xc, h0, w_a, b_a, w_x, b_x, lam):
    B, T, C = xc.shape
    xf = xc.astype(jnp.float32)
    xb = xf.reshape(B, T, N_LRU_BLOCKS, LRU_BLOCK)
    r = jax.nn.sigmoid(jnp.einsum('btnc,ncd->btnd', xb, w_a.astype(jnp.float32)).reshape(B, T, C) + b_a)
    i = jax.nn.sigmoid(jnp.einsum('btnc,ncd->btnd', xb, w_x.astype(jnp.float32)).reshape(B, T, C) + b_x)
    log_a = -LRU_C * r * jax.nn.softplus(-lam.astype(jnp.float32))
    a = jnp.exp(log_a)
    u = jnp.sqrt(-jnp.expm1(2.0 * log_a)) * (i * xf)

    def step(h, inp):
        a_t, u_t = inp
        h = a_t * h + u_t
        return h, h

    h_last, hs = lax.scan(step, h0.astype(jnp.float32), (jnp.swapaxes(a, 0, 1), jnp.swapaxes(u, 0, 1)))
    return jnp.swapaxes(hs, 0, 1), h_last


def peer(x, w_query, sub_keys, expert_u, expert_v):
    N = x.shape[0]
    n_pad = -(-N // TOKEN_BLOCK) * TOKEN_BLOCK
    xp = jnp.pad(x, ((0, n_pad - N), (0, 0))).reshape(n_pad // TOKEN_BLOCK, TOKEN_BLOCK, D_MODEL)

    def block(xb):
        q = (xb @ w_query).reshape(TOKEN_BLOCK, PEER_HEADS, 2, HALF_KEY)
        s = jnp.einsum('thpe,hpke->thpk', q, sub_keys, preferred_element_type=jnp.float32)
        sv, si = lax.top_k(s, TOPK)
        cand = sv[:, :, 0, :, None] + sv[:, :, 1, None, :]
        cand_idx = si[:, :, 0, :, None] * N_KEYS + si[:, :, 1, None, :]
        fv, fi = lax.top_k(cand.reshape(TOKEN_BLOCK, PEER_HEADS, TOPK * TOPK), TOPK)
        eidx = jnp.take_along_axis(cand_idx.reshape(TOKEN_BLOCK, PEER_HEADS, TOPK * TOPK), fi, axis=-1)
        g = jax.nn.softmax(fv, axis=-1)
        hid = jax.nn.gelu(jnp.einsum('thkd,td->thk', expert_u[eidx], xb, preferred_element_type=jnp.float32), approximate=False)
        return jnp.einsum('thk,thkd->td', (g * hid).astype(xb.dtype), expert_v[eidx])

    return lax.map(block, xp).reshape(n_pad, D_MODEL)[:N]


def trunk_layer(x, k_buf, v_buf, conv_buf, h0, w_in, conv_w, conv_b, lru_w_a, lru_b_a, lru_w_x, lru_b_x,
                lru_lambda, w_out, ln1_g, ln1_b, peer_w_query, peer_sub_keys, peer_u, peer_v, ln2_g, ln2_b):
    B, T, _ = x.shape
    proj = x @ w_in
    q, k, v, xr, gate = jnp.split(proj, SPLITS, axis=-1)
    q = q.reshape(B, T, N_ATT_HEADS, HEAD_DIM)
    k = k.reshape(B, T, N_ATT_HEADS, HEAD_DIM)
    v = v.reshape(B, T, N_ATT_HEADS, HEAD_DIM)
    slopes = alibi_slopes()
    if k_buf is None:
        parts = [dilated_prompt(q, k, v, wd, dl, slopes) for wd, dl in PATTERNS]
        keep = min(MAX_WINDOW, T)
        new_k, new_v = k[:, T - keep:], v[:, T - keep:]
        conv_buf = jnp.zeros((B, CONV_WIDTH - 1, D_LRU), xr.dtype)
        h0 = jnp.zeros((B, D_LRU), jnp.float32)
    else:
        k_cat = jnp.concatenate([k_buf.astype(k.dtype), k], axis=1)
        v_cat = jnp.concatenate([v_buf.astype(v.dtype), v], axis=1)
        parts = [dilated_sample(q, k_cat, v_cat, wd, dl, slopes) for wd, dl in PATTERNS]
        new_k, new_v = k_cat[:, T:], v_cat[:, T:]
    att = merge_patterns(parts).reshape(B, T, D_ATT).astype(x.dtype)
    x_ext = jnp.concatenate([conv_buf.astype(xr.dtype), xr], axis=1)
    xc = causal_conv(x_ext, conv_w, conv_b)
    new_conv = x_ext[:, T:]
    hs, h_last = rg_lru(xc, h0, lru_w_a, lru_b_a, lru_w_x, lru_b_x, lru_lambda)
    rec = (hs * jax.nn.gelu(gate.astype(jnp.float32), approximate=False)).astype(x.dtype)
    mix = jnp.concatenate([att, rec], axis=-1) @ w_out
    x1 = layer_norm(ALPHA * x + mix, ln1_g, ln1_b)
    ffn = peer(x1.reshape(B * T, D_MODEL), peer_w_query, peer_sub_keys, peer_u, peer_v).reshape(B, T, D_MODEL)
    y = layer_norm(ALPHA * x1 + ffn, ln2_g, ln2_b)
    return y, new_k, new_v, new_conv, h_last


def setup_inputs(seed: int = 0) -> dict:
    key = jax.random.key(seed)
    ks = jax.random.split(key, 24)
    f32 = jnp.float32

    def nrm(k, shape, scale=1.0):
        return jax.random.normal(k, shape, f32) * scale

    u = jax.random.uniform(ks[10], (DEPTH, D_LRU), f32, 0.9, 0.999)
    a0 = u ** (1.0 / LRU_C)
    lam = jnp.log(a0) - jnp.log1p(-a0)
    return {
        'x_prompt': nrm(ks[0], (BATCH, SEQ, D_MODEL)),
        'x_sample': nrm(ks[1], (DEC_BATCH, DEC_SEQ, D_MODEL)),
        'cache_k': nrm(ks[2], (DEPTH, DEC_BATCH, WIN_BUF, N_ATT_HEADS, HEAD_DIM)),
        'cache_v': nrm(ks[3], (DEPTH, DEC_BATCH, WIN_BUF, N_ATT_HEADS, HEAD_DIM)),
        'state_conv': nrm(ks[4], (DEPTH, DEC_BATCH, CONV_WIDTH - 1, D_LRU)),
        'state_h': nrm(ks[5], (DEPTH, DEC_BATCH, D_LRU), 0.5),
        'w_in': nrm(ks[6], (DEPTH, D_MODEL, D_IN), D_MODEL ** -0.5),
        'conv_w': nrm(ks[7], (DEPTH, CONV_WIDTH, D_LRU), 0.5),
        'conv_b': nrm(ks[8], (DEPTH, D_LRU), 0.01),
        'lru_w_a': nrm(ks[9], (DEPTH, N_LRU_BLOCKS, LRU_BLOCK, LRU_BLOCK), LRU_BLOCK ** -0.5),
        'lru_b_a': nrm(ks[11], (DEPTH, D_LRU), 0.01),
        'lru_w_x': nrm(ks[12], (DEPTH, N_LRU_BLOCKS, LRU_BLOCK, LRU_BLOCK), LRU_BLOCK ** -0.5),
        'lru_b_x': nrm(ks[13], (DEPTH, D_LRU), 0.01),
        'lru_lambda': lam,
        'w_out': nrm(ks[14], (DEPTH, D_MODEL, D_MODEL), BETA * D_MODEL ** -0.5),
        'ln1_g': 1.0 + nrm(ks[15], (DEPTH, D_MODEL), 0.02),
        'ln1_b': nrm(ks[16], (DEPTH, D_MODEL), 0.02),
        'peer_w_query': nrm(ks[17], (DEPTH, D_MODEL, PEER_HEADS * D_KEY), D_MODEL ** -0.5),
        'peer_sub_keys': nrm(ks[18], (DEPTH, PEER_HEADS, 2, N_KEYS, HALF_KEY), HALF_KEY ** -0.5),
        'peer_u': nrm(ks[19], (DEPTH, N_EXPERTS, D_MODEL), D_MODEL ** -0.5),
        'peer_v': nrm(ks[20], (DEPTH, N_EXPERTS, D_MODEL), BETA * PEER_HEADS ** -0.5),
        'ln2_g': 1.0 + nrm(ks[21], (DEPTH, D_MODEL), 0.02),
        'ln2_b': nrm(ks[22], (DEPTH, D_MODEL), 0.02),
    }


def reference(x_prompt, x_sample, cache_k, cache_v, state_conv, state_h, w_in, conv_w, conv_b, lru_w_a,
              lru_b_a, lru_w_x, lru_b_x, lru_lambda, w_out, ln1_g, ln1_b, peer_w_query, peer_sub_keys,
              peer_u, peer_v, ln2_g, ln2_b):
    yp, ys = x_prompt, x_sample
    kp, vp, cp, hp = [], [], [], []
    kq, vq, cq, hq = [], [], [], []
    for layer in range(DEPTH):
        wts = (w_in[layer], conv_w[layer], conv_b[layer], lru_w_a[layer], lru_b_a[layer], lru_w_x[layer],
               lru_b_x[layer], lru_lambda[layer], w_out[layer], ln1_g[layer], ln1_b[layer],
               peer_w_query[layer], peer_sub_keys[layer], peer_u[layer], peer_v[layer], ln2_g[layer], ln2_b[layer])
        yp, k1, v1, c1, h1 = trunk_layer(yp, None, None, None, None, *wts)
        ys, k2, v2, c2, h2 = trunk_layer(ys, cache_k[layer], cache_v[layer], state_conv[layer], state_h[layer], *wts)
        kp.append(k1); vp.append(v1); cp.append(c1); hp.append(h1)
        kq.append(k2); vq.append(v2); cq.append(c2); hq.append(h2)
    return (yp, ys, jnp.stack(kp), jnp.stack(vp), jnp.stack(cp), jnp.stack(hp),
            jnp.stack(kq), jnp.stack(vq), jnp.stack(cq), jnp.stack(hq))
```

```python
import functools

import jax
import jax.numpy as jnp
from jax import lax
from jax.experimental import pallas as pl
from jax.experimental.pallas import tpu as pltpu

F32 = jnp.float32
BF16 = jnp.bfloat16

N_ATT_HEADS = 12
HEAD_DIM = 128
D_ATT = N_ATT_HEADS * HEAD_DIM
ATT_SCALE = HEAD_DIM ** -0.5
PATTERNS = ((128, 1), (512, 4), (2048, 16))
N_LRU_BLOCKS = 8
CONV_WIDTH = 4
LRU_C = 8.0
N_KEYS = 128
PEER_HEADS = 8
HALF_KEY = 128
TOPK = 16
DEPTH = 1
ALPHA = (2.0 * DEPTH) ** 0.25
LN_EPS = 1e-5

V7X_SUBLANES = 8
V7X_LANES = 128
V7X_VMEM_LIMIT_BYTES = 56 * 1024 * 1024

ROW_TILE = 256
ATT_Q_BLOCK = 256
ROUTE_TILE = 128
GATHER_TOKENS = 8
N_PICKS = PEER_HEADS * TOPK


def _params(*sem):
    return pltpu.CompilerParams(dimension_semantics=sem, vmem_limit_bytes=V7X_VMEM_LIMIT_BYTES)


def _resident(shape):
    return pl.BlockSpec(shape, lambda *_: (0,) * len(shape), pipeline_mode=pl.Buffered(1))


def _layer_norm(z, g, b):
    mu = jnp.mean(z, axis=-1, keepdims=True)
    zc = z - mu
    var = jnp.mean(zc * zc, axis=-1, keepdims=True)
    return zc * lax.rsqrt(var + LN_EPS) * g + b


def _gelu(x):
    return 0.5 * x * (1.0 + lax.erf(x * (2.0 ** -0.5)))


def _in_proj_kernel(x_ref, w_ref, q_ref, k_ref, v_ref, xr_ref, gate_ref):
    xb = x_ref[...].astype(BF16)
    col = 0
    for o_ref in (q_ref, k_ref, v_ref):
        res = jnp.dot(xb, w_ref[:, col:col + D_ATT], preferred_element_type=F32)
        nb, _, tr, _ = o_ref.shape
        for bb in range(nb):
            for h in range(N_ATT_HEADS):
                o_ref[bb, h] = res[bb * tr:(bb + 1) * tr, h * HEAD_DIM:(h + 1) * HEAD_DIM]
        col += D_ATT
    for o_ref in (xr_ref, gate_ref):
        width = o_ref.shape[-1]
        o_ref[...] = jnp.dot(xb, w_ref[:, col:col + width], preferred_element_type=F32)
        col += width


def _in_proj(x2d, w_bf16, batch, t_len, d_lru):
    m, d_model = x2d.shape
    tm = min(ROW_TILE, m)
    tr = min(tm, t_len)
    nb = tm // tr
    tiles_per_batch = t_len // tr
    head_blk = pl.BlockSpec((nb, N_ATT_HEADS, tr, HEAD_DIM),
                            lambda i: (i // tiles_per_batch, 0, i % tiles_per_batch, 0))
    head_shape = jax.ShapeDtypeStruct((batch, N_ATT_HEADS, t_len, HEAD_DIM), F32)
    return pl.pallas_call(
        _in_proj_kernel,
        grid=(m // tm,),
        in_specs=[pl.BlockSpec((tm, d_model), lambda i: (i, 0)), _resident(w_bf16.shape)],
        out_specs=[head_blk] * 3 + [pl.BlockSpec((tm, d_lru), lambda i: (i, 0))] * 2,
        out_shape=[head_shape] * 3 + [jax.ShapeDtypeStruct((m, d_lru), F32)] * 2,
        compiler_params=_params("arbitrary"),
    )(x2d, w_bf16)


def _pattern_softmax(sb, delta, window, dil):
    valid = (delta >= 0) & (delta <= window) & ((delta & (dil - 1)) == 0)
    sp = jnp.where(valid, sb, -jnp.inf)
    m = jnp.max(sp, axis=-1, keepdims=True)
    return sp, m


def _merge_patterns(ms, ls, os_):
    m_max = functools.reduce(jnp.maximum, ms)
    ws = [jnp.exp(m - m_max) for m in ms]
    num = sum(w * o for w, o in zip(ws, os_))
    den = sum(w * l for w, l in zip(ws, ls))
    return num / den


def _qk(q, k):
    return lax.dot_general(q.astype(BF16), k.astype(BF16), (((1,), (1,)), ((), ())),
                           preferred_element_type=F32) * ATT_SCALE


def _attn_prompt_kernel(slopes_ref, q_ref, k_ref, v_ref, o_ref):
    seq = q_ref.shape[2]
    slope = slopes_ref[pl.program_id(1)]
    qb = min(ATT_Q_BLOCK, seq)
    for i in range(seq // qb):
        q0, hi = i * qb, (i + 1) * qb
        s = _qk(q_ref[0, 0, q0:hi, :], k_ref[0, 0, 0:hi, :])
        delta = (q0 + lax.broadcasted_iota(jnp.int32, (qb, hi), 0)
                 - lax.broadcasted_iota(jnp.int32, (qb, hi), 1))
        sb = s - slope * delta.astype(F32)
        ms, ls, os_ = [], [], []
        for window, dil in PATTERNS:
            lo = max(0, q0 - window)
            sp, m = _pattern_softmax(sb[:, lo:hi], delta[:, lo:hi], window, dil)
            p = jnp.exp(sp - m)
            ms.append(m)
            ls.append(jnp.sum(p, axis=-1, keepdims=True))
            os_.append(jnp.dot(p.astype(BF16), v_ref[0, 0, lo:hi, :].astype(BF16),
                               preferred_element_type=F32))
        o_ref[0, q0:hi, :] = _merge_patterns(ms, ls, os_)


def _attn_prompt(slopes, q, k, v):
    b, _, seq, _ = q.shape
    blk = pl.BlockSpec((1, 1, seq, HEAD_DIM), lambda bi, h, *_: (bi, h, 0, 0))
    return pl.pallas_call(
        _attn_prompt_kernel,
        grid_spec=pltpu.PrefetchScalarGridSpec(
            num_scalar_prefetch=1, grid=(b, N_ATT_HEADS),
            in_specs=[blk, blk, blk],
            out_specs=pl.BlockSpec((1, seq, HEAD_DIM), lambda bi, h, *_: (bi, 0, h))),
        out_shape=jax.ShapeDtypeStruct((b, seq, D_ATT), F32),
        compiler_params=_params("arbitrary", "arbitrary"),
    )(slopes, q, k, v)


def _attn_sample_kernel(slopes_ref, qn_ref, kn_ref, vn_ref, ck_ref, cv_ref,
                        att_ref, ok_ref, ov_ref, *, heads_per_step):
    t_new = qn_ref.shape[2]
    n_past = ck_ref.shape[2]
    for c_ref, n_ref, o_ref in ((ck_ref, kn_ref, ok_ref), (cv_ref, vn_ref, ov_ref)):
        o_ref[0, :, 0:n_past - t_new, :] = c_ref[0, :, t_new:n_past, :]
        o_ref[0, :, n_past - t_new:n_past, :] = n_ref[0]
    t_row = lax.broadcasted_iota(jnp.int32, (t_new, n_past), 0)
    d_old = n_past + t_row - lax.broadcasted_iota(jnp.int32, (t_new, n_past), 1)
    d_new = (lax.broadcasted_iota(jnp.int32, (t_new, t_new), 0)
             - lax.broadcasted_iota(jnp.int32, (t_new, t_new), 1))
    for hh in range(heads_per_step):
        slope = slopes_ref[pl.program_id(1) * heads_per_step + hh]
        cols = slice(hh * HEAD_DIM, (hh + 1) * HEAD_DIM)
        q = qn_ref[0, hh]
        sb_old = _qk(q, ck_ref[0, hh]) - slope * d_old.astype(F32)
        sb_new = _qk(q, kn_ref[0, hh]) - slope * d_new.astype(F32)
        v_old = cv_ref[0, hh].astype(BF16)
        v_new = vn_ref[0, hh].astype(BF16)
        ms, ls, os_ = [], [], []
        for window, dil in PATTERNS:
            sp_old, m_old = _pattern_softmax(sb_old, d_old, window, dil)
            sp_new, m_new = _pattern_softmax(sb_new, d_new, window, dil)
            m = jnp.maximum(m_old, m_new)
            p_old = jnp.exp(sp_old - m)
            p_new = jnp.exp(sp_new - m)
            ms.append(m)
            ls.append(jnp.sum(p_old, axis=-1, keepdims=True) + jnp.sum(p_new, axis=-1, keepdims=True))
            os_.append(jnp.dot(p_old.astype(BF16), v_old, preferred_element_type=F32)
                       + jnp.dot(p_new.astype(BF16), v_new, preferred_element_type=F32))
        att_ref[0, :, cols] = _merge_patterns(ms, ls, os_)


def _attn_sample(slopes, qn, kn, vn, cache_k, cache_v):
    b, _, t_new, _ = qn.shape
    n_past = cache_k.shape[2]
    hps = 2
    new_blk = pl.BlockSpec((1, hps, t_new, HEAD_DIM), lambda bi, j, *_: (bi, j, 0, 0))
    buf_blk = pl.BlockSpec((1, hps, n_past, HEAD_DIM), lambda bi, j, *_: (bi, j, 0, 0))
    att_blk = pl.BlockSpec((1, t_new, hps * HEAD_DIM), lambda bi, j, *_: (bi, 0, j))
    return pl.pallas_call(
        functools.partial(_attn_sample_kernel, heads_per_step=hps),
        grid_spec=pltpu.PrefetchScalarGridSpec(
            num_scalar_prefetch=1, grid=(b, N_ATT_HEADS // hps),
            in_specs=[new_blk, new_blk, new_blk, buf_blk, buf_blk],
            out_specs=[att_blk, buf_blk, buf_blk]),
        out_shape=[jax.ShapeDtypeStruct((b, t_new, D_ATT), F32),
                   jax.ShapeDtypeStruct(cache_k.shape, F32),
                   jax.ShapeDtypeStruct(cache_v.shape, F32)],
        compiler_params=_params("arbitrary", "arbitrary"),
    )(slopes, qn, kn, vn, cache_k, cache_v)


def _recurrent_kernel(xr_ref, gate_ref, cs_ref, h0_ref, cw_ref, cb_ref, wa_ref, ba_ref,
                      wx_ref, bx_ref, lam_ref, rec_ref, nc_ref, hl_ref,
                      xext_ref, a_ref, u_ref, *, chunk):
    t_len = xr_ref.shape[1]
    pad = V7X_SUBLANES
    hist = CONV_WIDTH - 1
    xext_ref[pad - hist:pad, :] = cs_ref[0]
    xext_ref[pad:pad + t_len, :] = xr_ref[0]
    nc_ref[0] = xext_ref[pad + t_len - hist:pad + t_len, :]
    z = -lam_ref[...]
    softplus = jnp.maximum(z, 0.0) + jnp.log(1.0 + jnp.exp(-jnp.abs(z)))
    for c in range(t_len // chunk):
        r0 = c * chunk
        xc = cb_ref[...]
        for j in range(CONV_WIDTH):
            xc = xc + xext_ref[pad - hist + j + r0:pad - hist + j + r0 + chunk, :] * cw_ref[j:j + 1, :]
        xcb = xc.astype(BF16)
        r = jax.nn.sigmoid(jnp.dot(xcb, wa_ref[...], preferred_element_type=F32) + ba_ref[...])
        i = jax.nn.sigmoid(jnp.dot(xcb, wx_ref[...], preferred_element_type=F32) + bx_ref[...])
        log_a = -LRU_C * r * softplus
        a = jnp.exp(log_a)
        a_ref[r0:r0 + chunk, :] = a
        u_ref[r0:r0 + chunk, :] = jnp.sqrt(1.0 - jnp.exp(2.0 * log_a)) * (i * xc)

    rows = V7X_SUBLANES

    def step(tile, h):
        base = pl.multiple_of(tile * rows, rows)
        a = a_ref[pl.ds(base, rows), :]
        u = u_ref[pl.ds(base, rows), :]
        out = []
        for s in range(rows):
            h = a[s:s + 1, :] * h + u[s:s + 1, :]
            out.append(h)
        a_ref[pl.ds(base, rows), :] = jnp.concatenate(out, axis=0)
        return h

    h_last = lax.fori_loop(0, t_len // rows, step, h0_ref[0])
    hl_ref[0] = h_last
    for c in range(t_len // chunk):
        r0 = c * chunk
        rec_ref[0, r0:r0 + chunk, :] = a_ref[r0:r0 + chunk, :] * _gelu(gate_ref[0, r0:r0 + chunk, :])


def _recurrent(xr, gate, conv_state, h0, conv_w, conv_b, wa_bd, b_a, wx_bd, b_x, lam):
    b, t_len, c = xr.shape
    chunk = min(ROW_TILE, t_len)
    hist = CONV_WIDTH - 1
    seq_blk = pl.BlockSpec((1, t_len, c), lambda bi: (bi, 0, 0))
    row = lambda n: pl.BlockSpec((1, n, c), lambda bi: (bi, 0, 0))
    vec = lambda a: pl.BlockSpec(a.shape, lambda bi: (0,) * a.ndim)
    weights = (conv_w, conv_b, wa_bd, b_a, wx_bd, b_x, lam)
    return pl.pallas_call(
        functools.partial(_recurrent_kernel, chunk=chunk),
        grid=(b,),
        in_specs=[seq_blk, seq_blk, row(hist), row(1)] + [vec(w) for w in weights],
        out_specs=[seq_blk, row(hist), row(1)],
        out_shape=[jax.ShapeDtypeStruct((b, t_len, c), F32),
                   jax.ShapeDtypeStruct((b, hist, c), F32),
                   jax.ShapeDtypeStruct((b, 1, c), F32)],
        scratch_shapes=[pltpu.VMEM((V7X_SUBLANES + t_len, c), F32),
                        pltpu.VMEM((t_len, c), F32), pltpu.VMEM((t_len, c), F32)],
        compiler_params=_params("arbitrary"),
    )(xr, gate, conv_state, h0, *weights)


def _out_proj_kernel(att_ref, rec_ref, x_ref, w_ref, g_ref, b_ref, o_ref):
    d_att = att_ref.shape[-1]
    mix = (jnp.dot(att_ref[...].astype(BF16), w_ref[0:d_att, :], preferred_element_type=F32)
           + jnp.dot(rec_ref[...].astype(BF16), w_ref[d_att:, :], preferred_element_type=F32))
    o_ref[...] = _layer_norm(ALPHA * x_ref[...] + mix, g_ref[...], b_ref[...])


def _out_proj(att, rec, x2d, w_bf16, g, b):
    m, d_model = x2d.shape
    tm = min(ROW_TILE, m)
    rows = lambda a: pl.BlockSpec((tm, a.shape[1]), lambda i: (i, 0))
    return pl.pallas_call(
        _out_proj_kernel,
        grid=(m // tm,),
        in_specs=[rows(att), rows(rec), rows(x2d), _resident(w_bf16.shape),
                  _resident(g.shape), _resident(b.shape)],
        out_specs=rows(x2d),
        out_shape=jax.ShapeDtypeStruct((m, d_model), F32),
        compiler_params=_params("arbitrary"),
    )(att, rec, x2d, w_bf16, g, b)


def _top_k_rows(vals, k):
    n = vals.shape[0]
    row = lax.broadcasted_iota(jnp.int32, vals.shape, 0)
    tv, ti = [], []
    for _ in range(k):
        m = jnp.max(vals, axis=0, keepdims=True)
        idx = jnp.min(jnp.where(vals == m, row, n), axis=0, keepdims=True)
        tv.append(m)
        ti.append(idx)
        vals = jnp.where(row == idx, -jnp.inf, vals)
    return jnp.concatenate(tv, axis=0), jnp.concatenate(ti, axis=0)


def _route_kernel(x_ref, wq_ref, keys_ref, eidx_ref, gate_ref):
    q = jnp.dot(x_ref[...].astype(BF16), wq_ref[...], preferred_element_type=F32)
    tokens = q.shape[0]
    e_rows, g_rows = [], []
    for h in range(PEER_HEADS):
        sv, si = [], []
        for p in range(2):
            c0 = (h * 2 + p) * HALF_KEY
            s = lax.dot_general(keys_ref[h, p].astype(BF16), q[:, c0:c0 + HALF_KEY].astype(BF16),
                                (((1,), (1,)), ((), ())), preferred_element_type=F32)
            v, i = _top_k_rows(s, TOPK)
            sv.append(v)
            si.append(i)
        cand = (sv[0][:, None, :] + sv[1][None, :, :]).reshape(TOPK * TOPK, tokens)
        cand_idx = (si[0][:, None, :] * N_KEYS + si[1][None, :, :]).reshape(TOPK * TOPK, tokens)
        fv, fi = _top_k_rows(cand, TOPK)
        flat = lax.broadcasted_iota(jnp.int32, cand.shape, 0)
        eidx = jnp.concatenate(
            [jnp.sum(jnp.where(flat == fi[r:r + 1, :], cand_idx, 0), axis=0, keepdims=True)
             for r in range(TOPK)], axis=0)
        ex = jnp.exp(fv - fv[0:1, :])
        e_rows.append(eidx)
        g_rows.append(ex / jnp.sum(ex, axis=0, keepdims=True))
    eidx_ref[...] = jnp.concatenate(e_rows, axis=0).T
    gate_ref[...] = jnp.concatenate(g_rows, axis=0).T


def _route(x1, wq_bf16, sub_keys):
    n, d_model = x1.shape
    tm = min(ROUTE_TILE, n)
    out_blk = pl.BlockSpec((tm, N_PICKS), lambda i: (i, 0))
    return pl.pallas_call(
        _route_kernel,
        grid=(n // tm,),
        in_specs=[pl.BlockSpec((tm, d_model), lambda i: (i, 0)), _resident(wq_bf16.shape),
                  _resident(sub_keys.shape)],
        out_specs=[out_blk, out_blk],
        out_shape=[jax.ShapeDtypeStruct((n, N_PICKS), jnp.int32),
                   jax.ShapeDtypeStruct((n, N_PICKS), F32)],
        compiler_params=_params("arbitrary"),
    )(x1, wq_bf16, sub_keys)


def _expert_kernel(idx_ref, idx_next_ref, x_ref, gate_ref, g_ref, b_ref, u_hbm, v_hbm, o_ref,
                   ubuf, vbuf, sems, *, n_steps):
    step = pl.program_id(0)
    tokens = x_ref.shape[0]
    rows = tokens * N_PICKS

    def copies(table, buf, slot, row, expert, sem):
        return pltpu.make_async_copy(table.at[pl.ds(expert, 1)], buf.at[slot, pl.ds(row, 1)], sem)

    def issue(idx, slot):
        def body(t, carry):
            for r in range(N_PICKS):
                e = idx[t, r]
                row = t * N_PICKS + r
                copies(u_hbm, ubuf, slot, row, e, sems.at[0, slot]).start()
                copies(v_hbm, vbuf, slot, row, e, sems.at[1, slot]).start()
            return carry
        lax.fori_loop(0, tokens, body, 0)

    @pl.when(step == 0)
    def _():
        issue(idx_ref, 0)

    @pl.when(step + 1 < n_steps)
    def _():
        issue(idx_next_ref, (step + 1) % 2)

    slot = step % 2
    pltpu.make_async_copy(u_hbm.at[pl.ds(0, rows)], ubuf.at[slot], sems.at[0, slot]).wait()
    pltpu.make_async_copy(v_hbm.at[pl.ds(0, rows)], vbuf.at[slot], sems.at[1, slot]).wait()

    outs = []
    for t in range(tokens):
        xt = x_ref[t:t + 1, :].astype(BF16)
        u_rows = ubuf[slot, t * N_PICKS:(t + 1) * N_PICKS, :].astype(BF16)
        hid = lax.dot_general(xt, u_rows, (((1,), (1,)), ((), ())), preferred_element_type=F32)
        w = (gate_ref[t:t + 1, :] * _gelu(hid)).astype(BF16)
        v_rows = vbuf[slot, t * N_PICKS:(t + 1) * N_PICKS, :].astype(BF16)
        outs.append(jnp.dot(w, v_rows, preferred_element_type=F32))
    ffn = jnp.concatenate(outs, axis=0)
    o_ref[...] = _layer_norm(ALPHA * x_ref[...] + ffn, g_ref[...], b_ref[...])


def _experts(eidx, gates, x1, g, b, expert_u, expert_v):
    n, d_model = x1.shape
    tokens = GATHER_TOKENS
    n_steps = n // tokens
    rows = tokens * N_PICKS
    smem_blk = lambda f: pl.BlockSpec((tokens, N_PICKS), f, memory_space=pltpu.SMEM)
    tok_blk = lambda w: pl.BlockSpec((tokens, w), lambda i: (i, 0))
    return pl.pallas_call(
        functools.partial(_expert_kernel, n_steps=n_steps),
        grid=(n_steps,),
        in_specs=[smem_blk(lambda i: (i, 0)),
                  smem_blk(lambda i: (jnp.minimum(i + 1, n_steps - 1), 0)),
                  tok_blk(d_model), tok_blk(N_PICKS), _resident(g.shape), _resident(b.shape),
                  pl.BlockSpec(memory_space=pl.ANY), pl.BlockSpec(memory_space=pl.ANY)],
        out_specs=tok_blk(d_model),
        out_shape=jax.ShapeDtypeStruct((n, d_model), F32),
        scratch_shapes=[pltpu.VMEM((2, rows, d_model), F32), pltpu.VMEM((2, rows, d_model), F32),
                        pltpu.SemaphoreType.DMA((2, 2))],
        compiler_params=_params("arbitrary"),
    )(eidx, eidx, x1, gates, g, b, expert_u, expert_v)


def _block_diag(w):
    n, c, d = w.shape
    eye = jnp.eye(n, dtype=w.dtype)
    return (eye[:, None, :, None] * w[:, :, None, :]).reshape(n * c, n * d)


def _trunk_layer(x, cache, conv_state, h0, wts):
    (w_in, conv_w, conv_b, w_a, b_a, w_x, b_x, lam, w_out, ln1_g, ln1_b,
     w_query, sub_keys, expert_u, expert_v, ln2_g, ln2_b) = wts
    b, t_len, d_model = x.shape
    d_lru = d_model - D_ATT
    x2d = x.reshape(b * t_len, d_model)
    slopes = 2.0 ** (-8.0 * jnp.arange(1, N_ATT_HEADS + 1, dtype=F32) / N_ATT_HEADS)

    head_major = lambda a: jnp.transpose(a, (0, 2, 1, 3))
    q, k, v, xr, gate = _in_proj(x2d, w_in.astype(BF16), b, t_len, d_lru)
    seq = lambda a: a.reshape(b, t_len, a.shape[-1])
    if cache is None:
        att = _attn_prompt(slopes, q, k, v)
        keep = min(PATTERNS[-1][0], t_len)
        new_k, new_v = k[:, :, t_len - keep:], v[:, :, t_len - keep:]
    else:
        att, new_k, new_v = _attn_sample(slopes, q, k, v, head_major(cache[0]), head_major(cache[1]))
    row = lambda a: a.reshape(1, -1)
    rec, new_conv, h_last = _recurrent(
        seq(xr), seq(gate), conv_state, h0.reshape(b, 1, d_lru), conv_w, row(conv_b),
        _block_diag(w_a).astype(BF16), row(b_a), _block_diag(w_x).astype(BF16), row(b_x), row(lam))
    x1 = _out_proj(att.reshape(b * t_len, D_ATT), rec.reshape(b * t_len, d_lru), x2d,
                   w_out.astype(BF16), row(ln1_g), row(ln1_b))
    eidx, gates = _route(x1, w_query.astype(BF16), sub_keys)
    y = _experts(eidx, gates, x1, row(ln2_g), row(ln2_b), expert_u, expert_v)
    return (y.reshape(b, t_len, d_model), head_major(new_k), head_major(new_v), new_conv,
            h_last.reshape(b, d_lru))


def kernel(x_prompt, x_sample, cache_k, cache_v, state_conv, state_h, w_in, conv_w, conv_b, lru_w_a, lru_b_a, lru_w_x, lru_b_x, lru_lambda, w_out, ln1_g, ln1_b, peer_w_query, peer_sub_keys, peer_u, peer_v, ln2_g, ln2_b):
    yp, ys = x_prompt, x_sample
    outs_p, outs_s = [], []
    for layer in range(w_in.shape[0]):
        wts = tuple(w[layer] for w in (
            w_in, conv_w, conv_b, lru_w_a, lru_b_a, lru_w_x, lru_b_x, lru_lambda, w_out, ln1_g,
            ln1_b, peer_w_query, peer_sub_keys, peer_u, peer_v, ln2_g, ln2_b))
        bp = yp.shape[0]
        d_lru = conv_w.shape[-1]
        yp, *rest_p = _trunk_layer(yp, None, jnp.zeros((bp, CONV_WIDTH - 1, d_lru), F32),
                                   jnp.zeros((bp, d_lru), F32), wts)
        ys, *rest_s = _trunk_layer(ys, (cache_k[layer], cache_v[layer]), state_conv[layer],
                                   state_h[layer], wts)
        outs_p.append(rest_p)
        outs_s.append(rest_s)
    stack = lambda outs, j: jnp.stack([o[j] for o in outs])
    return (yp, ys, *(stack(outs_p, j) for j in range(4)), *(stack(outs_s, j) for j in range(4)))
```

```python
import functools

import jax
import jax.numpy as jnp
from jax import lax
from jax.experimental import pallas as pl
from jax.experimental.pallas import tpu as pltpu

F32 = jnp.float32
BF16 = jnp.bfloat16

N_ATT_HEADS = 12
HEAD_DIM = 128
D_ATT = N_ATT_HEADS * HEAD_DIM
ATT_SCALE = HEAD_DIM ** -0.5
PATTERNS = ((128, 1), (512, 4), (2048, 16))
N_LRU_BLOCKS = 8
CONV_WIDTH = 4
LRU_C = 8.0
N_KEYS = 128
PEER_HEADS = 8
HALF_KEY = 128
TOPK = 16
DEPTH = 1
ALPHA = (2.0 * DEPTH) ** 0.25
LN_EPS = 1e-5

V7X_SUBLANES = 8
V7X_LANES = 128
V7X_VMEM_LIMIT_BYTES = 56 * 1024 * 1024

ROW_TILE = 256
ATT_Q_BLOCK = 256
ROUTE_TILE = 128
GATHER_TOKENS = 8
N_PICKS = PEER_HEADS * TOPK


def _params(*sem):
    return pltpu.CompilerParams(dimension_semantics=sem, vmem_limit_bytes=V7X_VMEM_LIMIT_BYTES)


def _resident(shape):
    return pl.BlockSpec(shape, lambda *_: (0,) * len(shape), pipeline_mode=pl.Buffered(1))


def _layer_norm(z, g, b):
    mu = jnp.mean(z, axis=-1, keepdims=True)
    zc = z - mu
    var = jnp.mean(zc * zc, axis=-1, keepdims=True)
    return zc * lax.rsqrt(var + LN_EPS) * g + b


def _gelu(x):
    return 0.5 * x * (1.0 + lax.erf(x * (2.0 ** -0.5)))


def _in_proj_kernel(x_ref, w_ref, q_ref, k_ref, v_ref, xr_ref, gate_ref):
    xb = x_ref[...].astype(BF16)
    col = 0
    for o_ref in (q_ref, k_ref, v_ref):
        res = jnp.dot(xb, w_ref[:, col:col + D_ATT], preferred_element_type=F32)
        nb, _, tr, _ = o_ref.shape
        for bb in range(nb):
            for h in range(N_ATT_HEADS):
                o_ref[bb, h] = res[bb * tr:(bb + 1) * tr, h * HEAD_DIM:(h + 1) * HEAD_DIM]
        col += D_ATT
    for o_ref in (xr_ref, gate_ref):
        width = o_ref.shape[-1]
        o_ref[...] = jnp.dot(xb, w_ref[:, col:col + width], preferred_element_type=F32)
        col += width


def _in_proj(x2d, w_bf16, batch, t_len, d_lru):
    m, d_model = x2d.shape
    tm = min(ROW_TILE, m)
    tr = min(tm, t_len)
    nb = tm // tr
    tiles_per_batch = t_len // tr
    head_blk = pl.BlockSpec((nb, N_ATT_HEADS, tr, HEAD_DIM),
                            lambda i: (i // tiles_per_batch, 0, i % tiles_per_batch, 0))
    head_shape = jax.ShapeDtypeStruct((batch, N_ATT_HEADS, t_len, HEAD_DIM), F32)
    return pl.pallas_call(
        _in_proj_kernel,
        grid=(m // tm,),
        in_specs=[pl.BlockSpec((tm, d_model), lambda i: (i, 0)), _resident(w_bf16.shape)],
        out_specs=[head_blk] * 3 + [pl.BlockSpec((tm, d_lru), lambda i: (i, 0))] * 2,
        out_shape=[head_shape] * 3 + [jax.ShapeDtypeStruct((m, d_lru), F32)] * 2,
        compiler_params=_params("arbitrary"),
    )(x2d, w_bf16)


def _pattern_softmax(sb, delta, window, dil):
    valid = (delta >= 0) & (delta <= window) & ((delta & (dil - 1)) == 0)
    sp = jnp.where(valid, sb, -jnp.inf)
    m = jnp.max(sp, axis=-1, keepdims=True)
    return sp, m


def _merge_patterns(ms, ls, os_):
    m_max = functools.reduce(jnp.maximum, ms)
    ws = [jnp.exp(m - m_max) for m in ms]
    num = sum(w * o for w, o in zip(ws, os_))
    den = sum(w * l for w, l in zip(ws, ls))
    return num / den


def _qk(q, k):
    return lax.dot_general(q.astype(BF16), k.astype(BF16), (((1,), (1,)), ((), ())),
                           preferred_element_type=F32) * ATT_SCALE


def _attn_prompt_kernel(slopes_ref, q_ref, k_ref, v_ref, o_ref):
    seq = q_ref.shape[2]
    slope = slopes_ref[pl.program_id(1)]
    qb = min(ATT_Q_BLOCK, seq)
    for i in range(seq // qb):
        q0, hi = i * qb, (i + 1) * qb
        s = _qk(q_ref[0, 0, q0:hi, :], k_ref[0, 0, 0:hi, :])
        delta = (q0 + lax.broadcasted_iota(jnp.int32, (qb, hi), 0)
                 - lax.broadcasted_iota(jnp.int32, (qb, hi), 1))
        sb = s - slope * delta.astype(F32)
        ms, ls, os_ = [], [], []
        for window, dil in PATTERNS:
            lo = max(0, q0 - window)
            sp, m = _pattern_softmax(sb[:, lo:hi], delta[:, lo:hi], window, dil)
            p = jnp.exp(sp - m)
            ms.append(m)
            ls.append(jnp.sum(p, axis=-1, keepdims=True))
            os_.append(jnp.dot(p.astype(BF16), v_ref[0, 0, lo:hi, :].astype(BF16),
                               preferred_element_type=F32))
        o_ref[0, q0:hi, :] = _merge_patterns(ms, ls, os_)


def _attn_prompt(slopes, q, k, v):
    b, _, seq, _ = q.shape
    blk = pl.BlockSpec((1, 1, seq, HEAD_DIM), lambda bi, h, *_: (bi, h, 0, 0))
    return pl.pallas_call(
        _attn_prompt_kernel,
        grid_spec=pltpu.PrefetchScalarGridSpec(
            num_scalar_prefetch=1, grid=(b, N_ATT_HEADS),
            in_specs=[blk, blk, blk],
            out_specs=pl.BlockSpec((1, seq, HEAD_DIM), lambda bi, h, *_: (bi, 0, h))),
        out_shape=jax.ShapeDtypeStruct((b, seq, D_ATT), F32),
        compiler_params=_params("arbitrary", "arbitrary"),
    )(slopes, q, k, v)


def _attn_sample_kernel(slopes_ref, qn_ref, kn_ref, vn_ref, ck_ref, cv_ref,
                        att_ref, ok_ref, ov_ref, *, heads_per_step):
    t_new = qn_ref.shape[2]
    n_past = ck_ref.shape[2]
    for c_ref, n_ref, o_ref in ((ck_ref, kn_ref, ok_ref), (cv_ref, vn_ref, ov_ref)):
        o_ref[0, :, 0:n_past - t_new, :] = c_ref[0, :, t_new:n_past, :]
        o_ref[0, :, n_past - t_new:n_past, :] = n_ref[0]
    t_row = lax.broadcasted_iota(jnp.int32, (t_new, n_past), 0)
    d_old = n_past + t_row - lax.broadcasted_iota(jnp.int32, (t_new, n_past), 1)
    d_new = (lax.broadcasted_iota(jnp.int32, (t_new, t_new), 0)
             - lax.broadcasted_iota(jnp.int32, (t_new, t_new), 1))
    for hh in range(heads_per_step):
        slope = slopes_ref[pl.program_id(1) * heads_per_step + hh]
        cols = slice(hh * HEAD_DIM, (hh + 1) * HEAD_DIM)
        q = qn_ref[0, hh]
        sb_old = _qk(q, ck_ref[0, hh]) - slope * d_old.astype(F32)
        sb_new = _qk(q, kn_ref[0, hh]) - slope * d_new.astype(F32)
        v_old = cv_ref[0, hh].astype(BF16)
        v_new = vn_ref[0, hh].astype(BF16)
        ms, ls, os_ = [], [], []
        for window, dil in PATTERNS:
            sp_old, m_old = _pattern_softmax(sb_old, d_old, window, dil)
            sp_new, m_new = _pattern_softmax(sb_new, d_new, window, dil)
            m = jnp.maximum(m_old, m_new)
            p_old = jnp.exp(sp_old - m)
            p_new = jnp.exp(sp_new - m)
            ms.append(m)
            ls.append(jnp.sum(p_old, axis=-1, keepdims=True) + jnp.sum(p_new, axis=-1, keepdims=True))
            os_.append(jnp.dot(p_old.astype(BF16), v_old, preferred_element_type=F32)
                       + jnp.dot(p_new.astype(BF16), v_new, preferred_element_type=F32))
        att_ref[0, :, cols] = _merge_patterns(ms, ls, os_)


def _attn_sample(slopes, qn, kn, vn, cache_k, cache_v):
    b, _, t_new, _ = qn.shape
    n_past = cache_k.shape[2]
    hps = 2
    new_blk = pl.BlockSpec((1, hps, t_new, HEAD_DIM), lambda bi, j, *_: (bi, j, 0, 0))
    buf_blk = pl.BlockSpec((1, hps, n_past, HEAD_DIM), lambda bi, j, *_: (bi, j, 0, 0))
    att_blk = pl.BlockSpec((1, t_new, hps * HEAD_DIM), lambda bi, j, *_: (bi, 0, j))
    return pl.pallas_call(
        functools.partial(_attn_sample_kernel, heads_per_step=hps),
        grid_spec=pltpu.PrefetchScalarGridSpec(
            num_scalar_prefetch=1, grid=(b, N_ATT_HEADS // hps),
            in_specs=[new_blk, new_blk, new_blk, buf_blk, buf_blk],
            out_specs=[att_blk, buf_blk, buf_blk]),
        out_shape=[jax.ShapeDtypeStruct((b, t_new, D_ATT), F32),
                   jax.ShapeDtypeStruct(cache_k.shape, F32),
                   jax.ShapeDtypeStruct(cache_v.shape, F32)],
        compiler_params=_params("arbitrary", "arbitrary"),
    )(slopes, qn, kn, vn, cache_k, cache_v)


def _recurrent_kernel(xr_ref, gate_ref, cs_ref, h0_ref, cw_ref, cb_ref, wa_ref, ba_ref,
                      wx_ref, bx_ref, lam_ref, rec_ref, nc_ref, hl_ref,
                      xext_ref, a_ref, u_ref, *, chunk):
    t_len = xr_ref.shape[1]
    pad = V7X_SUBLANES
    hist = CONV_WIDTH - 1
    xext_ref[pad - hist:pad, :] = cs_ref[0]
    xext_ref[pad:pad + t_len, :] = xr_ref[0]
    nc_ref[0] = xext_ref[pad + t_len - hist:pad + t_len, :]
    z = -lam_ref[...]
    softplus = jnp.maximum(z, 0.0) + jnp.log(1.0 + jnp.exp(-jnp.abs(z)))
    for c in range(t_len // chunk):
        r0 = c * chunk
        xc = cb_ref[...]
        for j in range(CONV_WIDTH):
            xc = xc + xext_ref[pad - hist + j + r0:pad - hist + j + r0 + chunk, :] * cw_ref[j:j + 1, :]
        xcb = xc.astype(BF16)
        r = jax.nn.sigmoid(jnp.dot(xcb, wa_ref[...], preferred_element_type=F32) + ba_ref[...])
        i = jax.nn.sigmoid(jnp.dot(xcb, wx_ref[...], preferred_element_type=F32) + bx_ref[...])
        log_a = -LRU_C * r * softplus
        a = jnp.exp(log_a)
        a_ref[r0:r0 + chunk, :] = a
        u_ref[r0:r0 + chunk, :] = jnp.sqrt(1.0 - jnp.exp(2.0 * log_a)) * (i * xc)

    rows = V7X_SUBLANES

    def step(tile, h):
        base = pl.multiple_of(tile * rows, rows)
        a = a_ref[pl.ds(base, rows), :]
        u = u_ref[pl.ds(base, rows), :]
        out = []
        for s in range(rows):
            h = a[s:s + 1, :] * h + u[s:s + 1, :]
            out.append(h)
        a_ref[pl.ds(base, rows), :] = jnp.concatenate(out, axis=0)
        return h

    h_last = lax.fori_loop(0, t_len // rows, step, h0_ref[0])
    hl_ref[0] = h_last
    for c in range(t_len // chunk):
        r0 = c * chunk
        rec_ref[0, r0:r0 + chunk, :] = a_ref[r0:r0 + chunk, :] * _gelu(gate_ref[0, r0:r0 + chunk, :])


def _recurrent(xr, gate, conv_state, h0, conv_w, conv_b, wa_bd, b_a, wx_bd, b_x, lam):
    b, t_len, c = xr.shape
    chunk = min(ROW_TILE, t_len)
    hist = CONV_WIDTH - 1
    seq_blk = pl.BlockSpec((1, t_len, c), lambda bi: (bi, 0, 0))
    row = lambda n: pl.BlockSpec((1, n, c), lambda bi: (bi, 0, 0))
    vec = lambda a: pl.BlockSpec(a.shape, lambda bi: (0,) * a.ndim)
    weights = (conv_w, conv_b, wa_bd, b_a, wx_bd, b_x, lam)
    return pl.pallas_call(
        functools.partial(_recurrent_kernel, chunk=chunk),
        grid=(b,),
        in_specs=[seq_blk, seq_blk, row(hist), row(1)] + [vec(w) for w in weights],
        out_specs=[seq_blk, row(hist), row(1)],
        out_shape=[jax.ShapeDtypeStruct((b, t_len, c), F32),
                   jax.ShapeDtypeStruct((b, hist, c), F32),
                   jax.ShapeDtypeStruct((b, 1, c), F32)],
        scratch_shapes=[pltpu.VMEM((V7X_SUBLANES + t_len, c), F32),
                        pltpu.VMEM((t_len, c), F32), pltpu.VMEM((t_len, c), F32)],
        compiler_params=_params("arbitrary"),
    )(xr, gate, conv_state, h0, *weights)


def _out_proj_kernel(att_ref, rec_ref, x_ref, w_ref, g_ref, b_ref, o_ref):
    d_att = att_ref.shape[-1]
    mix = (jnp.dot(att_ref[...].astype(BF16), w_ref[0:d_att, :], preferred_element_type=F32)
           + jnp.dot(rec_ref[...].astype(BF16), w_ref[d_att:, :], preferred_element_type=F32))
    o_ref[...] = _layer_norm(ALPHA * x_ref[...] + mix, g_ref[...], b_ref[...])


def _out_proj(att, rec, x2d, w_bf16, g, b):
    m, d_model = x2d.shape
    tm = min(ROW_TILE, m)
    rows = lambda a: pl.BlockSpec((tm, a.shape[1]), lambda i: (i, 0))
    return pl.pallas_call(
        _out_proj_kernel,
        grid=(m // tm,),
        in_specs=[rows(att), rows(rec), rows(x2d), _resident(w_bf16.shape),
                  _resident(g.shape), _resident(b.shape)],
        out_specs=rows(x2d),
        out_shape=jax.ShapeDtypeStruct((m, d_model), F32),
        compiler_params=_params("arbitrary"),
    )(att, rec, x2d, w_bf16, g, b)


def _top_k_rows(vals, k):
    n = vals.shape[0]
    row = lax.broadcasted_iota(jnp.int32, vals.shape, 0)
    tv, ti = [], []
    for _ in range(k):
        m = jnp.max(vals, axis=0, keepdims=True)
        idx = jnp.min(jnp.where(vals == m, row, n), axis=0, keepdims=True)
        tv.append(m)
        ti.append(idx)
        vals = jnp.where(row == idx, -jnp.inf, vals)
    return jnp.concatenate(tv, axis=0), jnp.concatenate(ti, axis=0)


def _route_kernel(x_ref, wq_ref, keys_ref, eidx_ref, gate_ref):
    q = jnp.dot(x_ref[...].astype(BF16), wq_ref[...], preferred_element_type=F32)
    tokens = q.shape[0]
    e_rows, g_rows = [], []
    for h in range(PEER_HEADS):
        sv, si = [], []
        for p in range(2):
            c0 = (h * 2 + p) * HALF_KEY
            s = lax.dot_general(keys_ref[h, p].astype(BF16), q[:, c0:c0 + HALF_KEY].astype(BF16),
                                (((1,), (1,)), ((), ())), preferred_element_type=F32)
            v, i = _top_k_rows(s, TOPK)
            sv.append(v)
            si.append(i)
        cand = (sv[0][:, None, :] + sv[1][None, :, :]).reshape(TOPK * TOPK, tokens)
        cand_idx = (si[0][:, None, :] * N_KEYS + si[1][None, :, :]).reshape(TOPK * TOPK, tokens)
        fv, fi = _top_k_rows(cand, TOPK)
        flat = lax.broadcasted_iota(jnp.int32, cand.shape, 0)
        eidx = jnp.concatenate(
            [jnp.sum(jnp.where(flat == fi[r:r + 1, :], cand_idx, 0), axis=0, keepdims=True)
             for r in range(TOPK)], axis=0)
        ex = jnp.exp(fv - fv[0:1, :])
        e_rows.append(eidx)
        g_rows.append(ex / jnp.sum(ex, axis=0, keepdims=True))
    eidx_ref[...] = jnp.concatenate(e_rows, axis=0).T
    gates = jnp.concatenate(g_rows, axis=0)
    for j in range(tokens // GATHER_TOKENS):
        gate_ref[j] = gates[:, j * GATHER_TOKENS:(j + 1) * GATHER_TOKENS]


def _route(x1, wq_bf16, sub_keys):
    n, d_model = x1.shape
    tm = min(ROUTE_TILE, n)
    groups = tm // GATHER_TOKENS
    return pl.pallas_call(
        _route_kernel,
        grid=(n // tm,),
        in_specs=[pl.BlockSpec((tm, d_model), lambda i: (i, 0)), _resident(wq_bf16.shape),
                  _resident(sub_keys.shape)],
        out_specs=[pl.BlockSpec((tm, N_PICKS), lambda i: (i, 0)),
                   pl.BlockSpec((groups, N_PICKS, GATHER_TOKENS), lambda i: (i, 0, 0))],
        out_shape=[jax.ShapeDtypeStruct((n, N_PICKS), jnp.int32),
                   jax.ShapeDtypeStruct((n // GATHER_TOKENS, N_PICKS, GATHER_TOKENS), F32)],
        compiler_params=_params("arbitrary"),
    )(x1, wq_bf16, sub_keys)


def _bf16_bits(a):
    return lax.bitcast_convert_type(a.astype(BF16), jnp.uint16).astype(jnp.uint32)


def _pack_experts(expert_u, expert_v):
    n, d = expert_u.shape
    half = d // 2
    words = [_bf16_bits(t[:, :half]) | (_bf16_bits(t[:, half:]) << 16) for t in (expert_u, expert_v)]
    return jnp.concatenate(words, axis=1).reshape(n, 1, d)


def _unpack_words(words):
    low = lax.bitcast_convert_type(words << 16, F32)
    high = lax.bitcast_convert_type(words & jnp.uint32(0xFFFF0000), F32)
    return low, high


def _expert_kernel(idx_ref, x_ref, gate_ref, g_ref, b_ref, tab_hbm, o_ref, buf0, buf1, sems, *,
                   n_blocks):
    step = pl.program_id(0)
    tokens, d_model = x_ref.shape
    half = d_model // 2
    bufs = (buf0, buf1)

    def slot_wait(slot):
        pltpu.make_async_copy(bufs[slot], bufs[slot], sems.at[slot]).wait()

    @pl.when(step == 0)
    def _zero_stand_in():
        buf1[...] = jnp.zeros_like(buf1)

    for slot in range(2):
        pl.when((step > 0) & (step % 2 != slot))(functools.partial(slot_wait, slot))

    def gather_and_finish(slot):
        for t in range(tokens):
            for r in range(N_PICKS):
                pltpu.make_async_copy(tab_hbm.at[idx_ref[t, r]],
                                      bufs[slot].at[pl.ds(t * N_PICKS + r, 1)],
                                      sems.at[slot]).start(priority=r % 2)
        rows = bufs[1 - slot]
        x = x_ref[...]
        hid_cols = []
        for t in range(tokens):
            u_low, u_high = _unpack_words(rows[t * N_PICKS:(t + 1) * N_PICKS, 0:half])
            prod = u_low * x[t:t + 1, 0:half] + u_high * x[t:t + 1, half:d_model]
            hid_cols.append(jnp.sum(prod, axis=-1, keepdims=True))
        w = gate_ref[0] * _gelu(jnp.concatenate(hid_cols, axis=1))
        outs = []
        for t in range(tokens):
            v_low, v_high = _unpack_words(rows[t * N_PICKS:(t + 1) * N_PICKS, half:d_model])
            wt = w[:, t:t + 1]
            outs.append(jnp.concatenate([jnp.sum(wt * v_low, axis=0, keepdims=True),
                                         jnp.sum(wt * v_high, axis=0, keepdims=True)], axis=1))
        ffn = jnp.concatenate(outs, axis=0)
        o_ref[...] = _layer_norm(ALPHA * x + ffn, g_ref[...], b_ref[...])

    for slot in range(2):
        pl.when(step % 2 == slot)(functools.partial(gather_and_finish, slot))

    @pl.when(step == n_blocks)
    def _drain():
        slot_wait(n_blocks % 2)


def _experts(eidx, gates, x1, g, b, table):
    n, d_model = x1.shape
    tokens = GATHER_TOKENS
    n_blocks = n // tokens
    prev = lambda i: (jnp.maximum(i - 1, 0), 0)
    tok_blk = lambda w: pl.BlockSpec((tokens, w), prev)
    return pl.pallas_call(
        functools.partial(_expert_kernel, n_blocks=n_blocks),
        grid=(n_blocks + 1,),
        in_specs=[pl.BlockSpec((tokens, N_PICKS), lambda i: (jnp.minimum(i, n_blocks - 1), 0),
                               memory_space=pltpu.SMEM),
                  tok_blk(d_model),
                  pl.BlockSpec((1, N_PICKS, tokens), lambda i: (jnp.maximum(i - 1, 0), 0, 0)),
                  _resident(g.shape), _resident(b.shape), pl.BlockSpec(memory_space=pl.ANY)],
        out_specs=tok_blk(d_model),
        out_shape=jax.ShapeDtypeStruct((n, d_model), F32),
        scratch_shapes=[pltpu.VMEM((tokens * N_PICKS, d_model), jnp.uint32),
                        pltpu.VMEM((tokens * N_PICKS, d_model), jnp.uint32),
                        pltpu.SemaphoreType.DMA((2,))],
        compiler_params=_params("arbitrary"),
    )(eidx, x1, gates, g, b, table)


def _block_diag(w):
    n, c, d = w.shape
    eye = jnp.eye(n, dtype=w.dtype)
    return (eye[:, None, :, None] * w[:, :, None, :]).reshape(n * c, n * d)


def _prepare_weights(w_in, conv_w, conv_b, w_a, b_a, w_x, b_x, lam, w_out, ln1_g, ln1_b,
                     w_query, sub_keys, expert_u, expert_v, ln2_g, ln2_b):
    return (w_in.astype(BF16), conv_w, conv_b, _block_diag(w_a).astype(BF16), b_a,
            _block_diag(w_x).astype(BF16), b_x, lam, w_out.astype(BF16), ln1_g, ln1_b,
            w_query.astype(BF16), sub_keys, _pack_experts(expert_u, expert_v), ln2_g, ln2_b)


def _trunk_layer(x, cache, conv_state, h0, wts):
    (w_in, conv_w, conv_b, wa_bd, b_a, wx_bd, b_x, lam, w_out, ln1_g, ln1_b,
     w_query, sub_keys, expert_table, ln2_g, ln2_b) = wts
    b, t_len, d_model = x.shape
    d_lru = d_model - D_ATT
    x2d = x.reshape(b * t_len, d_model)
    slopes = 2.0 ** (-8.0 * jnp.arange(1, N_ATT_HEADS + 1, dtype=F32) / N_ATT_HEADS)

    head_major = lambda a: jnp.transpose(a, (0, 2, 1, 3))
    q, k, v, xr, gate = _in_proj(x2d, w_in, b, t_len, d_lru)
    seq = lambda a: a.reshape(b, t_len, a.shape[-1])
    if cache is None:
        att = _attn_prompt(slopes, q, k, v)
        keep = min(PATTERNS[-1][0], t_len)
        new_k, new_v = k[:, :, t_len - keep:], v[:, :, t_len - keep:]
    else:
        att, new_k, new_v = _attn_sample(slopes, q, k, v, head_major(cache[0]), head_major(cache[1]))
    row = lambda a: a.reshape(1, -1)
    rec, new_conv, h_last = _recurrent(
        seq(xr), seq(gate), conv_state, h0.reshape(b, 1, d_lru), conv_w, row(conv_b),
        wa_bd, row(b_a), wx_bd, row(b_x), row(lam))
    x1 = _out_proj(att.reshape(b * t_len, D_ATT), rec.reshape(b * t_len, d_lru), x2d,
                   w_out, row(ln1_g), row(ln1_b))
    eidx, gates2 = _route(x1, w_query, sub_keys)
    y = _experts(eidx, gates2, x1, row(ln2_g), row(ln2_b), expert_table)
    return (y.reshape(b, t_len, d_model), head_major(new_k), head_major(new_v), new_conv,
            h_last.reshape(b, d_lru))


def kernel(x_prompt, x_sample, cache_k, cache_v, state_conv, state_h, w_in, conv_w, conv_b, lru_w_a, lru_b_a, lru_w_x, lru_b_x, lru_lambda, w_out, ln1_g, ln1_b, peer_w_query, peer_sub_keys, peer_u, peer_v, ln2_g, ln2_b):
    yp, ys = x_prompt, x_sample
    outs_p, outs_s = [], []
    for layer in range(w_in.shape[0]):
        wts = _prepare_weights(*(w[layer] for w in (
            w_in, conv_w, conv_b, lru_w_a, lru_b_a, lru_w_x, lru_b_x, lru_lambda, w_out, ln1_g,
            ln1_b, peer_w_query, peer_sub_keys, peer_u, peer_v, ln2_g, ln2_b)))
        bp = yp.shape[0]
        d_lru = conv_w.shape[-1]
        yp, *rest_p = _trunk_layer(yp, None, jnp.zeros((bp, CONV_WIDTH - 1, d_lru), F32),
                                   jnp.zeros((bp, d_lru), F32), wts)
        ys, *rest_s = _trunk_layer(ys, (cache_k[layer], cache_v[layer]), state_conv[layer],
                                   state_h[layer], wts)
        outs_p.append(rest_p)
        outs_s.append(rest_s)
    stack = lambda outs, j: jnp.stack([o[j] for o in outs])
    return (yp, ys, *(stack(outs_p, j) for j in range(4)), *(stack(outs_s, j) for j in range(4)))
```

```python
import functools

import jax
import jax.numpy as jnp
from jax import lax
from jax.experimental import pallas as pl
from jax.experimental.pallas import tpu as pltpu

F32 = jnp.float32
BF16 = jnp.bfloat16

N_ATT_HEADS = 12
HEAD_DIM = 128
D_ATT = N_ATT_HEADS * HEAD_DIM
ATT_SCALE = HEAD_DIM ** -0.5
PATTERNS = ((128, 1), (512, 4), (2048, 16))
N_LRU_BLOCKS = 8
CONV_WIDTH = 4
LRU_C = 8.0
N_KEYS = 128
PEER_HEADS = 8
HALF_KEY = 128
TOPK = 16
DEPTH = 1
ALPHA = (2.0 * DEPTH) ** 0.25
LN_EPS = 1e-5

V7X_SUBLANES = 8
V7X_LANES = 128
V7X_VMEM_LIMIT_BYTES = 56 * 1024 * 1024

ROW_TILE = 256
ATT_Q_BLOCK = 256
ROUTE_TILE = 128
GATHER_TOKENS = 8
GATHER_LAG = 2
N_PICKS = PEER_HEADS * TOPK


def _params(*sem):
    return pltpu.CompilerParams(dimension_semantics=sem, vmem_limit_bytes=V7X_VMEM_LIMIT_BYTES)


def _resident(shape):
    return pl.BlockSpec(shape, lambda *_: (0,) * len(shape), pipeline_mode=pl.Buffered(1))


def _layer_norm(z, g, b):
    mu = jnp.mean(z, axis=-1, keepdims=True)
    zc = z - mu
    var = jnp.mean(zc * zc, axis=-1, keepdims=True)
    return zc * lax.rsqrt(var + LN_EPS) * g + b


def _gelu(x):
    return 0.5 * x * (1.0 + lax.erf(x * (2.0 ** -0.5)))


def _in_proj_kernel(x_ref, w_ref, q_ref, k_ref, v_ref, xr_ref, gate_ref):
    xb = x_ref[...].astype(BF16)
    col = 0
    for o_ref in (q_ref, k_ref, v_ref):
        res = jnp.dot(xb, w_ref[:, col:col + D_ATT], preferred_element_type=F32)
        nb, _, tr, _ = o_ref.shape
        for bb in range(nb):
            for h in range(N_ATT_HEADS):
                o_ref[bb, h] = res[bb * tr:(bb + 1) * tr, h * HEAD_DIM:(h + 1) * HEAD_DIM]
        col += D_ATT
    for o_ref in (xr_ref, gate_ref):
        width = o_ref.shape[-1]
        o_ref[...] = jnp.dot(xb, w_ref[:, col:col + width], preferred_element_type=F32)
        col += width


def _in_proj(x2d, w_bf16, batch, t_len, d_lru):
    m, d_model = x2d.shape
    tm = min(ROW_TILE, m)
    tr = min(tm, t_len)
    nb = tm // tr
    tiles_per_batch = t_len // tr
    head_blk = pl.BlockSpec((nb, N_ATT_HEADS, tr, HEAD_DIM),
                            lambda i: (i // tiles_per_batch, 0, i % tiles_per_batch, 0))
    head_shape = jax.ShapeDtypeStruct((batch, N_ATT_HEADS, t_len, HEAD_DIM), F32)
    return pl.pallas_call(
        _in_proj_kernel,
        grid=(m // tm,),
        in_specs=[pl.BlockSpec((tm, d_model), lambda i: (i, 0)), _resident(w_bf16.shape)],
        out_specs=[head_blk] * 3 + [pl.BlockSpec((tm, d_lru), lambda i: (i, 0))] * 2,
        out_shape=[head_shape] * 3 + [jax.ShapeDtypeStruct((m, d_lru), F32)] * 2,
        compiler_params=_params("arbitrary"),
    )(x2d, w_bf16)


def _pattern_softmax(sb, delta, window, dil):
    valid = (delta >= 0) & (delta <= window) & ((delta & (dil - 1)) == 0)
    sp = jnp.where(valid, sb, -jnp.inf)
    m = jnp.max(sp, axis=-1, keepdims=True)
    return sp, m


def _merge_patterns(ms, ls, os_):
    m_max = functools.reduce(jnp.maximum, ms)
    ws = [jnp.exp(m - m_max) for m in ms]
    num = sum(w * o for w, o in zip(ws, os_))
    den = sum(w * l for w, l in zip(ws, ls))
    return num / den


def _qk(q, k):
    return lax.dot_general(q.astype(BF16), k.astype(BF16), (((1,), (1,)), ((), ())),
                           preferred_element_type=F32) * ATT_SCALE


def _attn_prompt_kernel(slopes_ref, q_ref, k_ref, v_ref, o_ref):
    seq = q_ref.shape[2]
    slope = slopes_ref[pl.program_id(1)]
    qb = min(ATT_Q_BLOCK, seq)
    for i in range(seq // qb):
        q0, hi = i * qb, (i + 1) * qb
        s = _qk(q_ref[0, 0, q0:hi, :], k_ref[0, 0, 0:hi, :])
        delta = (q0 + lax.broadcasted_iota(jnp.int32, (qb, hi), 0)
                 - lax.broadcasted_iota(jnp.int32, (qb, hi), 1))
        sb = s - slope * delta.astype(F32)
        ms, ls, os_ = [], [], []
        for window, dil in PATTERNS:
            lo = max(0, q0 - window)
            sp, m = _pattern_softmax(sb[:, lo:hi], delta[:, lo:hi], window, dil)
            p = jnp.exp(sp - m)
            ms.append(m)
            ls.append(jnp.sum(p, axis=-1, keepdims=True))
            os_.append(jnp.dot(p.astype(BF16), v_ref[0, 0, lo:hi, :].astype(BF16),
                               preferred_element_type=F32))
        o_ref[0, q0:hi, :] = _merge_patterns(ms, ls, os_)


def _attn_prompt(slopes, q, k, v):
    b, _, seq, _ = q.shape
    blk = pl.BlockSpec((1, 1, seq, HEAD_DIM), lambda bi, h, *_: (bi, h, 0, 0))
    return pl.pallas_call(
        _attn_prompt_kernel,
        grid_spec=pltpu.PrefetchScalarGridSpec(
            num_scalar_prefetch=1, grid=(b, N_ATT_HEADS),
            in_specs=[blk, blk, blk],
            out_specs=pl.BlockSpec((1, seq, HEAD_DIM), lambda bi, h, *_: (bi, 0, h))),
        out_shape=jax.ShapeDtypeStruct((b, seq, D_ATT), F32),
        compiler_params=_params("arbitrary", "arbitrary"),
    )(slopes, q, k, v)


def _attn_sample_kernel(slopes_ref, qn_ref, kn_ref, vn_ref, ck_ref, cv_ref,
                        att_ref, ok_ref, ov_ref, *, heads_per_step):
    t_new = qn_ref.shape[2]
    n_past = ck_ref.shape[2]
    for c_ref, n_ref, o_ref in ((ck_ref, kn_ref, ok_ref), (cv_ref, vn_ref, ov_ref)):
        o_ref[0, :, 0:n_past - t_new, :] = c_ref[0, :, t_new:n_past, :]
        o_ref[0, :, n_past - t_new:n_past, :] = n_ref[0]
    t_row = lax.broadcasted_iota(jnp.int32, (t_new, n_past), 0)
    d_old = n_past + t_row - lax.broadcasted_iota(jnp.int32, (t_new, n_past), 1)
    d_new = (lax.broadcasted_iota(jnp.int32, (t_new, t_new), 0)
             - lax.broadcasted_iota(jnp.int32, (t_new, t_new), 1))
    for hh in range(heads_per_step):
        slope = slopes_ref[pl.program_id(1) * heads_per_step + hh]
        cols = slice(hh * HEAD_DIM, (hh + 1) * HEAD_DIM)
        q = qn_ref[0, hh]
        sb_old = _qk(q, ck_ref[0, hh]) - slope * d_old.astype(F32)
        sb_new = _qk(q, kn_ref[0, hh]) - slope * d_new.astype(F32)
        v_old = cv_ref[0, hh].astype(BF16)
        v_new = vn_ref[0, hh].astype(BF16)
        ms, ls, os_ = [], [], []
        for window, dil in PATTERNS:
            sp_old, m_old = _pattern_softmax(sb_old, d_old, window, dil)
            sp_new, m_new = _pattern_softmax(sb_new, d_new, window, dil)
            m = jnp.maximum(m_old, m_new)
            p_old = jnp.exp(sp_old - m)
            p_new = jnp.exp(sp_new - m)
            ms.append(m)
            ls.append(jnp.sum(p_old, axis=-1, keepdims=True) + jnp.sum(p_new, axis=-1, keepdims=True))
            os_.append(jnp.dot(p_old.astype(BF16), v_old, preferred_element_type=F32)
                       + jnp.dot(p_new.astype(BF16), v_new, preferred_element_type=F32))
        att_ref[0, :, cols] = _merge_patterns(ms, ls, os_)


def _attn_sample(slopes, qn, kn, vn, cache_k, cache_v):
    b, _, t_new, _ = qn.shape
    n_past = cache_k.shape[2]
    hps = 2
    new_blk = pl.BlockSpec((1, hps, t_new, HEAD_DIM), lambda bi, j, *_: (bi, j, 0, 0))
    buf_blk = pl.BlockSpec((1, hps, n_past, HEAD_DIM), lambda bi, j, *_: (bi, j, 0, 0))
    att_blk = pl.BlockSpec((1, t_new, hps * HEAD_DIM), lambda bi, j, *_: (bi, 0, j))
    return pl.pallas_call(
        functools.partial(_attn_sample_kernel, heads_per_step=hps),
        grid_spec=pltpu.PrefetchScalarGridSpec(
            num_scalar_prefetch=1, grid=(b, N_ATT_HEADS // hps),
            in_specs=[new_blk, new_blk, new_blk, buf_blk, buf_blk],
            out_specs=[att_blk, buf_blk, buf_blk]),
        out_shape=[jax.ShapeDtypeStruct((b, t_new, D_ATT), F32),
                   jax.ShapeDtypeStruct(cache_k.shape, F32),
                   jax.ShapeDtypeStruct(cache_v.shape, F32)],
        compiler_params=_params("arbitrary", "arbitrary"),
    )(slopes, qn, kn, vn, cache_k, cache_v)


def _recurrent_kernel(xr_ref, gate_ref, cs_ref, h0_ref, cw_ref, cb_ref, wa_ref, ba_ref,
                      wx_ref, bx_ref, lam_ref, rec_ref, nc_ref, hl_ref,
                      xext_ref, a_ref, u_ref, *, chunk):
    t_len = xr_ref.shape[1]
    pad = V7X_SUBLANES
    hist = CONV_WIDTH - 1
    xext_ref[pad - hist:pad, :] = cs_ref[0]
    xext_ref[pad:pad + t_len, :] = xr_ref[0]
    nc_ref[0] = xext_ref[pad + t_len - hist:pad + t_len, :]
    z = -lam_ref[...]
    softplus = jnp.maximum(z, 0.0) + jnp.log(1.0 + jnp.exp(-jnp.abs(z)))
    for c in range(t_len // chunk):
        r0 = c * chunk
        xc = cb_ref[...]
        for j in range(CONV_WIDTH):
            xc = xc + xext_ref[pad - hist + j + r0:pad - hist + j + r0 + chunk, :] * cw_ref[j:j + 1, :]
        xcb = xc.astype(BF16)
        r = jax.nn.sigmoid(jnp.dot(xcb, wa_ref[...], preferred_element_type=F32) + ba_ref[...])
        i = jax.nn.sigmoid(jnp.dot(xcb, wx_ref[...], preferred_element_type=F32) + bx_ref[...])
        log_a = -LRU_C * r * softplus
        a = jnp.exp(log_a)
        a_ref[r0:r0 + chunk, :] = a
        u_ref[r0:r0 + chunk, :] = jnp.sqrt(1.0 - jnp.exp(2.0 * log_a)) * (i * xc)

    rows = V7X_SUBLANES

    def step(tile, h):
        base = pl.multiple_of(tile * rows, rows)
        a = a_ref[pl.ds(base, rows), :]
        u = u_ref[pl.ds(base, rows), :]
        out = []
        for s in range(rows):
            h = a[s:s + 1, :] * h + u[s:s + 1, :]
            out.append(h)
        a_ref[pl.ds(base, rows), :] = jnp.concatenate(out, axis=0)
        return h

    h_last = lax.fori_loop(0, t_len // rows, step, h0_ref[0])
    hl_ref[0] = h_last
    for c in range(t_len // chunk):
        r0 = c * chunk
        rec_ref[0, r0:r0 + chunk, :] = a_ref[r0:r0 + chunk, :] * _gelu(gate_ref[0, r0:r0 + chunk, :])


def _recurrent(xr, gate, conv_state, h0, conv_w, conv_b, wa_bd, b_a, wx_bd, b_x, lam):
    b, t_len, c = xr.shape
    chunk = min(ROW_TILE, t_len)
    hist = CONV_WIDTH - 1
    seq_blk = pl.BlockSpec((1, t_len, c), lambda bi: (bi, 0, 0))
    row = lambda n: pl.BlockSpec((1, n, c), lambda bi: (bi, 0, 0))
    vec = lambda a: pl.BlockSpec(a.shape, lambda bi: (0,) * a.ndim)
    weights = (conv_w, conv_b, wa_bd, b_a, wx_bd, b_x, lam)
    return pl.pallas_call(
        functools.partial(_recurrent_kernel, chunk=chunk),
        grid=(b,),
        in_specs=[seq_blk, seq_blk, row(hist), row(1)] + [vec(w) for w in weights],
        out_specs=[seq_blk, row(hist), row(1)],
        out_shape=[jax.ShapeDtypeStruct((b, t_len, c), F32),
                   jax.ShapeDtypeStruct((b, hist, c), F32),
                   jax.ShapeDtypeStruct((b, 1, c), F32)],
        scratch_shapes=[pltpu.VMEM((V7X_SUBLANES + t_len, c), F32),
                        pltpu.VMEM((t_len, c), F32), pltpu.VMEM((t_len, c), F32)],
        compiler_params=_params("arbitrary"),
    )(xr, gate, conv_state, h0, *weights)


def _out_proj_kernel(att_ref, rec_ref, x_ref, w_ref, g_ref, b_ref, o_ref):
    d_att = att_ref.shape[-1]
    mix = (jnp.dot(att_ref[...].astype(BF16), w_ref[0:d_att, :], preferred_element_type=F32)
           + jnp.dot(rec_ref[...].astype(BF16), w_ref[d_att:, :], preferred_element_type=F32))
    o_ref[...] = _layer_norm(ALPHA * x_ref[...] + mix, g_ref[...], b_ref[...])


def _out_proj(att, rec, x2d, w_bf16, g, b):
    m, d_model = x2d.shape
    tm = min(ROW_TILE, m)
    rows = lambda a: pl.BlockSpec((tm, a.shape[1]), lambda i: (i, 0))
    return pl.pallas_call(
        _out_proj_kernel,
        grid=(m // tm,),
        in_specs=[rows(att), rows(rec), rows(x2d), _resident(w_bf16.shape),
                  _resident(g.shape), _resident(b.shape)],
        out_specs=rows(x2d),
        out_shape=jax.ShapeDtypeStruct((m, d_model), F32),
        compiler_params=_params("arbitrary"),
    )(att, rec, x2d, w_bf16, g, b)


def _top_k_rows(vals, k):
    n = vals.shape[0]
    row = lax.broadcasted_iota(jnp.int32, vals.shape, 0)
    tv, ti = [], []
    for _ in range(k):
        m = jnp.max(vals, axis=0, keepdims=True)
        idx = jnp.min(jnp.where(vals == m, row, n), axis=0, keepdims=True)
        tv.append(m)
        ti.append(idx)
        vals = jnp.where(row == idx, -jnp.inf, vals)
    return jnp.concatenate(tv, axis=0), jnp.concatenate(ti, axis=0)


def _route_kernel(x_ref, wq_ref, keys_ref, eidx_ref, gate_ref):
    q = jnp.dot(x_ref[...].astype(BF16), wq_ref[...], preferred_element_type=F32)
    tokens = q.shape[0]
    e_rows, g_rows = [], []
    for h in range(PEER_HEADS):
        sv, si = [], []
        for p in range(2):
            c0 = (h * 2 + p) * HALF_KEY
            s = lax.dot_general(keys_ref[h, p].astype(BF16), q[:, c0:c0 + HALF_KEY].astype(BF16),
                                (((1,), (1,)), ((), ())), preferred_element_type=F32)
            v, i = _top_k_rows(s, TOPK)
            sv.append(v)
            si.append(i)
        widths = [TOPK // (a + 1) for a in range(TOPK)]
        cand = jnp.concatenate([sv[0][a:a + 1] + sv[1][0:w] for a, w in enumerate(widths)], axis=0)
        cand_idx = jnp.concatenate([si[0][a:a + 1] * N_KEYS + si[1][0:w]
                                    for a, w in enumerate(widths)], axis=0)
        pad = -cand.shape[0] % V7X_SUBLANES
        cand = jnp.concatenate([cand, jnp.full((pad, tokens), -jnp.inf, F32)], axis=0)
        cand_idx = jnp.concatenate([cand_idx, jnp.zeros((pad, tokens), jnp.int32)], axis=0)
        fv, fi = _top_k_rows(cand, TOPK)
        flat = lax.broadcasted_iota(jnp.int32, cand.shape, 0)
        eidx = jnp.concatenate(
            [jnp.sum(jnp.where(flat == fi[r:r + 1, :], cand_idx, 0), axis=0, keepdims=True)
             for r in range(TOPK)], axis=0)
        ex = jnp.exp(fv - fv[0:1, :])
        e_rows.append(eidx)
        g_rows.append(ex / jnp.sum(ex, axis=0, keepdims=True))
    eidx_ref[...] = jnp.concatenate(e_rows, axis=0).T
    gates = jnp.concatenate(g_rows, axis=0)
    for j in range(tokens // GATHER_TOKENS):
        gate_ref[j] = gates[:, j * GATHER_TOKENS:(j + 1) * GATHER_TOKENS]


def _route(x1, wq_bf16, sub_keys):
    n, d_model = x1.shape
    tm = min(ROUTE_TILE, n)
    groups = tm // GATHER_TOKENS
    return pl.pallas_call(
        _route_kernel,
        grid=(n // tm,),
        in_specs=[pl.BlockSpec((tm, d_model), lambda i: (i, 0)), _resident(wq_bf16.shape),
                  _resident(sub_keys.shape)],
        out_specs=[pl.BlockSpec((tm, N_PICKS), lambda i: (i, 0)),
                   pl.BlockSpec((groups, N_PICKS, GATHER_TOKENS), lambda i: (i, 0, 0))],
        out_shape=[jax.ShapeDtypeStruct((n, N_PICKS), jnp.int32),
                   jax.ShapeDtypeStruct((n // GATHER_TOKENS, N_PICKS, GATHER_TOKENS), F32)],
        compiler_params=_params("arbitrary"),
    )(x1, wq_bf16, sub_keys)


def _bf16_bits(a):
    return lax.bitcast_convert_type(a.astype(BF16), jnp.uint16).astype(jnp.uint32)


def _pack_experts(expert_u, expert_v):
    n, d = expert_u.shape
    half = d // 2
    words = [_bf16_bits(t[:, :half]) | (_bf16_bits(t[:, half:]) << 16) for t in (expert_u, expert_v)]
    return jnp.concatenate(words, axis=1).reshape(n, 1, d)


def _unpack_words(words):
    low = lax.bitcast_convert_type(words << 16, F32)
    high = lax.bitcast_convert_type(words & jnp.uint32(0xFFFF0000), F32)
    return low, high


def _expert_kernel(idx_ref, x_ref, gate_ref, g_ref, b_ref, tab_hbm, o_ref, *scratch, n_blocks):
    step = pl.program_id(0)
    tokens, d_model = x_ref.shape
    half = d_model // 2
    *bufs, sems = scratch
    n_slots = len(bufs)

    def slot_wait(slot):
        pltpu.make_async_copy(bufs[slot], bufs[slot], sems.at[slot]).wait()

    @pl.when(step == 0)
    def _zero_stand_ins():
        for buf in bufs[1:]:
            buf[...] = jnp.zeros_like(buf)

    for slot in range(n_slots):
        pl.when((step >= GATHER_LAG) & ((step - GATHER_LAG) % n_slots == slot))(
            functools.partial(slot_wait, slot))

    def gather_and_finish(slot):
        for t in range(tokens):
            for r in range(N_PICKS):
                pltpu.make_async_copy(tab_hbm.at[idx_ref[t, r]],
                                      bufs[slot].at[pl.ds(t * N_PICKS + r, 1)],
                                      sems.at[slot]).start(priority=r % 2)
        rows = bufs[(slot - GATHER_LAG) % n_slots]
        x = x_ref[...]
        hid_cols = []
        for t in range(tokens):
            u_low, u_high = _unpack_words(rows[t * N_PICKS:(t + 1) * N_PICKS, 0:half])
            prod = u_low * x[t:t + 1, 0:half] + u_high * x[t:t + 1, half:d_model]
            hid_cols.append(jnp.sum(prod, axis=-1, keepdims=True))
        w = gate_ref[0] * _gelu(jnp.concatenate(hid_cols, axis=1))
        outs = []
        for t in range(tokens):
            v_low, v_high = _unpack_words(rows[t * N_PICKS:(t + 1) * N_PICKS, half:d_model])
            wt = w[:, t:t + 1]
            outs.append(jnp.concatenate([jnp.sum(wt * v_low, axis=0, keepdims=True),
                                         jnp.sum(wt * v_high, axis=0, keepdims=True)], axis=1))
        ffn = jnp.concatenate(outs, axis=0)
        o_ref[...] = _layer_norm(ALPHA * x + ffn, g_ref[...], b_ref[...])

    for slot in range(n_slots):
        pl.when(step % n_slots == slot)(functools.partial(gather_and_finish, slot))

    @pl.when(step == n_blocks + GATHER_LAG - 1)
    def _drain():
        for lag in range(GATHER_LAG):
            slot_wait((n_blocks + lag) % n_slots)


def _experts(eidx, gates, x1, g, b, table):
    n, d_model = x1.shape
    tokens = GATHER_TOKENS
    n_blocks = n // tokens
    n_slots = GATHER_LAG + 1
    done = lambda i: jnp.maximum(i - GATHER_LAG, 0)
    tok_blk = lambda w: pl.BlockSpec((tokens, w), lambda i: (done(i), 0))
    return pl.pallas_call(
        functools.partial(_expert_kernel, n_blocks=n_blocks),
        grid=(n_blocks + GATHER_LAG,),
        in_specs=[pl.BlockSpec((tokens, N_PICKS), lambda i: (jnp.minimum(i, n_blocks - 1), 0),
                               memory_space=pltpu.SMEM),
                  tok_blk(d_model),
                  pl.BlockSpec((1, N_PICKS, tokens), lambda i: (done(i), 0, 0)),
                  _resident(g.shape), _resident(b.shape), pl.BlockSpec(memory_space=pl.ANY)],
        out_specs=tok_blk(d_model),
        out_shape=jax.ShapeDtypeStruct((n, d_model), F32),
        scratch_shapes=[pltpu.VMEM((tokens * N_PICKS, d_model), jnp.uint32)] * n_slots
                       + [pltpu.SemaphoreType.DMA((n_slots,))],
        compiler_params=_params("arbitrary"),
    )(eidx, x1, gates, g, b, table)


def _block_diag(w):
    n, c, d = w.shape
    eye = jnp.eye(n, dtype=w.dtype)
    return (eye[:, None, :, None] * w[:, :, None, :]).reshape(n * c, n * d)


def _prepare_weights(w_in, conv_w, conv_b, w_a, b_a, w_x, b_x, lam, w_out, ln1_g, ln1_b,
                     w_query, sub_keys, expert_u, expert_v, ln2_g, ln2_b):
    return (w_in.astype(BF16), conv_w, conv_b, _block_diag(w_a).astype(BF16), b_a,
            _block_diag(w_x).astype(BF16), b_x, lam, w_out.astype(BF16), ln1_g, ln1_b,
            w_query.astype(BF16), sub_keys, _pack_experts(expert_u, expert_v), ln2_g, ln2_b)


def _trunk_layer(x, cache, conv_state, h0, wts):
    (w_in, conv_w, conv_b, wa_bd, b_a, wx_bd, b_x, lam, w_out, ln1_g, ln1_b,
     w_query, sub_keys, expert_table, ln2_g, ln2_b) = wts
    b, t_len, d_model = x.shape
    d_lru = d_model - D_ATT
    x2d = x.reshape(b * t_len, d_model)
    slopes = 2.0 ** (-8.0 * jnp.arange(1, N_ATT_HEADS + 1, dtype=F32) / N_ATT_HEADS)

    head_major = lambda a: jnp.transpose(a, (0, 2, 1, 3))
    q, k, v, xr, gate = _in_proj(x2d, w_in, b, t_len, d_lru)
    seq = lambda a: a.reshape(b, t_len, a.shape[-1])
    if cache is None:
        att = _attn_prompt(slopes, q, k, v)
        keep = min(PATTERNS[-1][0], t_len)
        new_k, new_v = k[:, :, t_len - keep:], v[:, :, t_len - keep:]
    else:
        att, new_k, new_v = _attn_sample(slopes, q, k, v, head_major(cache[0]), head_major(cache[1]))
    row = lambda a: a.reshape(1, -1)
    rec, new_conv, h_last = _recurrent(
        seq(xr), seq(gate), conv_state, h0.reshape(b, 1, d_lru), conv_w, row(conv_b),
        wa_bd, row(b_a), wx_bd, row(b_x), row(lam))
    x1 = _out_proj(att.reshape(b * t_len, D_ATT), rec.reshape(b * t_len, d_lru), x2d,
                   w_out, row(ln1_g), row(ln1_b))
    eidx, gates2 = _route(x1, w_query, sub_keys)
    y = _experts(eidx, gates2, x1, row(ln2_g), row(ln2_b), expert_table)
    return (y.reshape(b, t_len, d_model), head_major(new_k), head_major(new_v), new_conv,
            h_last.reshape(b, d_lru))


def kernel(x_prompt, x_sample, cache_k, cache_v, state_conv, state_h, w_in, conv_w, conv_b, lru_w_a, lru_b_a, lru_w_x, lru_b_x, lru_lambda, w_out, ln1_g, ln1_b, peer_w_query, peer_sub_keys, peer_u, peer_v, ln2_g, ln2_b):
    yp, ys = x_prompt, x_sample
    outs_p, outs_s = [], []
    for layer in range(w_in.shape[0]):
        wts = _prepare_weights(*(w[layer] for w in (
            w_in, conv_w, conv_b, lru_w_a, lru_b_a, lru_w_x, lru_b_x, lru_lambda, w_out, ln1_g,
            ln1_b, peer_w_query, peer_sub_keys, peer_u, peer_v, ln2_g, ln2_b)))
        bp = yp.shape[0]
        d_lru = conv_w.shape[-1]
        yp, *rest_p = _trunk_layer(yp, None, jnp.zeros((bp, CONV_WIDTH - 1, d_lru), F32),
                                   jnp.zeros((bp, d_lru), F32), wts)
        ys, *rest_s = _trunk_layer(ys, (cache_k[layer], cache_v[layer]), state_conv[layer],
                                   state_h[layer], wts)
        outs_p.append(rest_p)
        outs_s.append(rest_s)
    stack = lambda outs, j: jnp.stack([o[j] for o in outs])
    return (yp, ys, *(stack(outs_p, j) for j in range(4)), *(stack(outs_s, j) for j in range(4)))
```

```python
import functools

import jax
import jax.numpy as jnp
from jax import lax
from jax.experimental import pallas as pl
from jax.experimental.pallas import tpu as pltpu

F32 = jnp.float32
BF16 = jnp.bfloat16

N_ATT_HEADS = 12
HEAD_DIM = 128
D_ATT = N_ATT_HEADS * HEAD_DIM
ATT_SCALE = HEAD_DIM ** -0.5
PATTERNS = ((128, 1), (512, 4), (2048, 16))
N_LRU_BLOCKS = 8
CONV_WIDTH = 4
LRU_C = 8.0
N_KEYS = 128
PEER_HEADS = 8
HALF_KEY = 128
TOPK = 16
DEPTH = 1
ALPHA = (2.0 * DEPTH) ** 0.25
LN_EPS = 1e-5

V7X_SUBLANES = 8
V7X_LANES = 128
V7X_VMEM_LIMIT_BYTES = 56 * 1024 * 1024

ROW_TILE = 256
ATT_BLOCK = 128
ROUTE_TILE = 128
GATHER_TOKENS = 8
GATHER_LAG = 2
N_PICKS = PEER_HEADS * TOPK


def _params(*sem):
    return pltpu.CompilerParams(dimension_semantics=sem, vmem_limit_bytes=V7X_VMEM_LIMIT_BYTES)


def _resident(shape):
    return pl.BlockSpec(shape, lambda *_: (0,) * len(shape), pipeline_mode=pl.Buffered(1))


def _layer_norm(z, g, b):
    mu = jnp.mean(z, axis=-1, keepdims=True)
    zc = z - mu
    var = jnp.mean(zc * zc, axis=-1, keepdims=True)
    return zc * lax.rsqrt(var + LN_EPS) * g + b


def _gelu(x):
    return 0.5 * x * (1.0 + lax.erf(x * (2.0 ** -0.5)))


def _in_proj_kernel(x_ref, w_ref, q_ref, k_ref, v_ref, xr_ref, gate_ref):
    xb = x_ref[...].astype(BF16)
    col = 0
    for o_ref in (q_ref, k_ref, v_ref):
        res = jnp.dot(xb, w_ref[:, col:col + D_ATT], preferred_element_type=F32)
        nb, _, tr, _ = o_ref.shape
        for bb in range(nb):
            for h in range(N_ATT_HEADS):
                o_ref[bb, h] = res[bb * tr:(bb + 1) * tr, h * HEAD_DIM:(h + 1) * HEAD_DIM]
        col += D_ATT
    for o_ref in (xr_ref, gate_ref):
        width = o_ref.shape[-1]
        o_ref[...] = jnp.dot(xb, w_ref[:, col:col + width], preferred_element_type=F32)
        col += width


def _in_proj(x2d, w_bf16, batch, t_len, d_lru):
    m, d_model = x2d.shape
    tm = min(ROW_TILE, m)
    tr = min(tm, t_len)
    nb = tm // tr
    tiles_per_batch = t_len // tr
    head_blk = pl.BlockSpec((nb, N_ATT_HEADS, tr, HEAD_DIM),
                            lambda i: (i // tiles_per_batch, 0, i % tiles_per_batch, 0))
    head_shape = jax.ShapeDtypeStruct((batch, N_ATT_HEADS, t_len, HEAD_DIM), F32)
    return pl.pallas_call(
        _in_proj_kernel,
        grid=(m // tm,),
        in_specs=[pl.BlockSpec((tm, d_model), lambda i: (i, 0)), _resident(w_bf16.shape)],
        out_specs=[head_blk] * 3 + [pl.BlockSpec((tm, d_lru), lambda i: (i, 0))] * 2,
        out_shape=[head_shape] * 3 + [jax.ShapeDtypeStruct((m, d_lru), F32)] * 2,
        compiler_params=_params("arbitrary"),
    )(x2d, w_bf16)


def _pattern_softmax(sb, delta, window, dil):
    valid = (delta >= 0) & (delta <= window) & ((delta & (dil - 1)) == 0)
    sp = jnp.where(valid, sb, -jnp.inf)
    m = jnp.max(sp, axis=-1, keepdims=True)
    return sp, m


def _merge_patterns(ms, ls, os_):
    m_max = functools.reduce(jnp.maximum, ms)
    ws = [jnp.exp(m - m_max) for m in ms]
    num = sum(w * o for w, o in zip(ws, os_))
    den = sum(w * l for w, l in zip(ws, ls))
    return num / den


def _qk(q, k):
    return lax.dot_general(q.astype(BF16), k.astype(BF16), (((1,), (1,)), ((), ())),
                           preferred_element_type=F32) * ATT_SCALE


def _attn_prompt_kernel(slopes_ref, q_ref, k_ref, v_ref, o_ref, m_scr, l_scr, acc_scr):
    seq = q_ref.shape[2]
    slope = slopes_ref[pl.program_id(1)]
    for p, (window, dil) in enumerate(PATTERNS):
        n_class = seq // dil
        reach = window // dil
        blk = min(ATT_BLOCK, n_class)
        for r in range(dil):
            def class_rows(ref, start, size):
                return ref[0, 0, pl.ds(r + dil * start, size, stride=dil), :] if dil > 1 else \
                    ref[0, 0, start:start + size, :]
            for c in range(n_class // blk):
                k0 = max(0, c * blk - reach)
                n_keys = (c + 1) * blk - k0
                s = _qk(class_rows(q_ref, c * blk, blk), class_rows(k_ref, k0, n_keys))
                dist = (c * blk - k0 + lax.broadcasted_iota(jnp.int32, (blk, n_keys), 0)
                        - lax.broadcasted_iota(jnp.int32, (blk, n_keys), 1))
                sp = jnp.where((dist >= 0) & (dist <= reach),
                               s - (slope * dil) * dist.astype(F32), -jnp.inf)
                m = jnp.max(sp, axis=-1, keepdims=True)
                prob = jnp.exp(sp - m)
                out = jnp.dot(prob.astype(BF16), class_rows(v_ref, k0, n_keys).astype(BF16),
                              preferred_element_type=F32)
                dst = pl.ds(r + dil * c * blk, blk, stride=dil) if dil > 1 else \
                    pl.ds(c * blk, blk)
                m_scr[p, dst, :] = jnp.broadcast_to(m, out.shape)
                l_scr[p, dst, :] = jnp.broadcast_to(jnp.sum(prob, axis=-1, keepdims=True), out.shape)
                acc_scr[p, dst, :] = out
    n_pat = len(PATTERNS)
    o_ref[0] = _merge_patterns([m_scr[p] for p in range(n_pat)], [l_scr[p] for p in range(n_pat)],
                               [acc_scr[p] for p in range(n_pat)])


def _attn_prompt(slopes, q, k, v):
    b, _, seq, _ = q.shape
    assert all(seq // d <= ATT_BLOCK or w // d <= ATT_BLOCK for w, d in PATTERNS)
    blk = pl.BlockSpec((1, 1, seq, HEAD_DIM), lambda bi, h, *_: (bi, h, 0, 0))
    stats = pltpu.VMEM((len(PATTERNS), seq, HEAD_DIM), F32)
    return pl.pallas_call(
        _attn_prompt_kernel,
        grid_spec=pltpu.PrefetchScalarGridSpec(
            num_scalar_prefetch=1, grid=(b, N_ATT_HEADS),
            in_specs=[blk, blk, blk],
            out_specs=pl.BlockSpec((1, seq, HEAD_DIM), lambda bi, h, *_: (bi, 0, h)),
            scratch_shapes=[stats, stats, stats]),
        out_shape=jax.ShapeDtypeStruct((b, seq, D_ATT), F32),
        compiler_params=_params("arbitrary", "arbitrary"),
    )(slopes, q, k, v)


def _attn_sample_kernel(slopes_ref, qn_ref, kn_ref, vn_ref, ck_ref, cv_ref,
                        att_ref, ok_ref, ov_ref, *, heads_per_step):
    t_new = qn_ref.shape[2]
    n_past = ck_ref.shape[2]
    for c_ref, n_ref, o_ref in ((ck_ref, kn_ref, ok_ref), (cv_ref, vn_ref, ov_ref)):
        o_ref[0, :, 0:n_past - t_new, :] = c_ref[0, :, t_new:n_past, :]
        o_ref[0, :, n_past - t_new:n_past, :] = n_ref[0]
    t_row = lax.broadcasted_iota(jnp.int32, (t_new, n_past), 0)
    d_old = n_past + t_row - lax.broadcasted_iota(jnp.int32, (t_new, n_past), 1)
    d_new = (lax.broadcasted_iota(jnp.int32, (t_new, t_new), 0)
             - lax.broadcasted_iota(jnp.int32, (t_new, t_new), 1))
    for hh in range(heads_per_step):
        slope = slopes_ref[pl.program_id(1) * heads_per_step + hh]
        cols = slice(hh * HEAD_DIM, (hh + 1) * HEAD_DIM)
        q = qn_ref[0, hh]
        sb_old = _qk(q, ck_ref[0, hh]) - slope * d_old.astype(F32)
        sb_new = _qk(q, kn_ref[0, hh]) - slope * d_new.astype(F32)
        v_old = cv_ref[0, hh].astype(BF16)
        v_new = vn_ref[0, hh].astype(BF16)
        ms, ls, os_ = [], [], []
        for window, dil in PATTERNS:
            sp_old, m_old = _pattern_softmax(sb_old, d_old, window, dil)
            sp_new, m_new = _pattern_softmax(sb_new, d_new, window, dil)
            m = jnp.maximum(m_old, m_new)
            p_old = jnp.exp(sp_old - m)
            p_new = jnp.exp(sp_new - m)
            ms.append(m)
            ls.append(jnp.sum(p_old, axis=-1, keepdims=True) + jnp.sum(p_new, axis=-1, keepdims=True))
            os_.append(jnp.dot(p_old.astype(BF16), v_old, preferred_element_type=F32)
                       + jnp.dot(p_new.astype(BF16), v_new, preferred_element_type=F32))
        att_ref[0, :, cols] = _merge_patterns(ms, ls, os_)


def _attn_sample(slopes, qn, kn, vn, cache_k, cache_v):
    b, _, t_new, _ = qn.shape
    n_past = cache_k.shape[2]
    hps = 2
    new_blk = pl.BlockSpec((1, hps, t_new, HEAD_DIM), lambda bi, j, *_: (bi, j, 0, 0))
    buf_blk = pl.BlockSpec((1, hps, n_past, HEAD_DIM), lambda bi, j, *_: (bi, j, 0, 0))
    att_blk = pl.BlockSpec((1, t_new, hps * HEAD_DIM), lambda bi, j, *_: (bi, 0, j))
    return pl.pallas_call(
        functools.partial(_attn_sample_kernel, heads_per_step=hps),
        grid_spec=pltpu.PrefetchScalarGridSpec(
            num_scalar_prefetch=1, grid=(b, N_ATT_HEADS // hps),
            in_specs=[new_blk, new_blk, new_blk, buf_blk, buf_blk],
            out_specs=[att_blk, buf_blk, buf_blk]),
        out_shape=[jax.ShapeDtypeStruct((b, t_new, D_ATT), F32),
                   jax.ShapeDtypeStruct(cache_k.shape, F32),
                   jax.ShapeDtypeStruct(cache_v.shape, F32)],
        compiler_params=_params("arbitrary", "arbitrary"),
    )(slopes, qn, kn, vn, cache_k, cache_v)


def _recurrent_kernel(xr_ref, gate_ref, cs_ref, h0_ref, cw_ref, cb_ref, wa_ref, ba_ref,
                      wx_ref, bx_ref, lam_ref, rec_ref, nc_ref, hl_ref,
                      xext_ref, a_ref, u_ref, *, chunk):
    t_len = xr_ref.shape[1]
    pad = V7X_SUBLANES
    hist = CONV_WIDTH - 1
    xext_ref[pad - hist:pad, :] = cs_ref[0]
    xext_ref[pad:pad + t_len, :] = xr_ref[0]
    nc_ref[0] = xext_ref[pad + t_len - hist:pad + t_len, :]
    z = -lam_ref[...]
    softplus = jnp.maximum(z, 0.0) + jnp.log(1.0 + jnp.exp(-jnp.abs(z)))
    for c in range(t_len // chunk):
        r0 = c * chunk
        xc = cb_ref[...]
        for j in range(CONV_WIDTH):
            xc = xc + xext_ref[pad - hist + j + r0:pad - hist + j + r0 + chunk, :] * cw_ref[j:j + 1, :]
        xcb = xc.astype(BF16)
        r = jax.nn.sigmoid(jnp.dot(xcb, wa_ref[...], preferred_element_type=F32) + ba_ref[...])
        i = jax.nn.sigmoid(jnp.dot(xcb, wx_ref[...], preferred_element_type=F32) + bx_ref[...])
        log_a = -LRU_C * r * softplus
        a = jnp.exp(log_a)
        a_ref[r0:r0 + chunk, :] = a
        u_ref[r0:r0 + chunk, :] = jnp.sqrt(1.0 - jnp.exp(2.0 * log_a)) * (i * xc)

    rows = V7X_SUBLANES

    def step(tile, h):
        base = pl.multiple_of(tile * rows, rows)
        a = a_ref[pl.ds(base, rows), :]
        u = u_ref[pl.ds(base, rows), :]
        out = []
        for s in range(rows):
            h = a[s:s + 1, :] * h + u[s:s + 1, :]
            out.append(h)
        a_ref[pl.ds(base, rows), :] = jnp.concatenate(out, axis=0)
        return h

    h_last = lax.fori_loop(0, t_len // rows, step, h0_ref[0])
    hl_ref[0] = h_last
    for c in range(t_len // chunk):
        r0 = c * chunk
        rec_ref[0, r0:r0 + chunk, :] = a_ref[r0:r0 + chunk, :] * _gelu(gate_ref[0, r0:r0 + chunk, :])


def _recurrent(xr, gate, conv_state, h0, conv_w, conv_b, wa_bd, b_a, wx_bd, b_x, lam):
    b, t_len, c = xr.shape
    chunk = min(ROW_TILE, t_len)
    hist = CONV_WIDTH - 1
    seq_blk = pl.BlockSpec((1, t_len, c), lambda bi: (bi, 0, 0))
    row = lambda n: pl.BlockSpec((1, n, c), lambda bi: (bi, 0, 0))
    vec = lambda a: pl.BlockSpec(a.shape, lambda bi: (0,) * a.ndim)
    weights = (conv_w, conv_b, wa_bd, b_a, wx_bd, b_x, lam)
    return pl.pallas_call(
        functools.partial(_recurrent_kernel, chunk=chunk),
        grid=(b,),
        in_specs=[seq_blk, seq_blk, row(hist), row(1)] + [vec(w) for w in weights],
        out_specs=[seq_blk, row(hist), row(1)],
        out_shape=[jax.ShapeDtypeStruct((b, t_len, c), F32),
                   jax.ShapeDtypeStruct((b, hist, c), F32),
                   jax.ShapeDtypeStruct((b, 1, c), F32)],
        scratch_shapes=[pltpu.VMEM((V7X_SUBLANES + t_len, c), F32),
                        pltpu.VMEM((t_len, c), F32), pltpu.VMEM((t_len, c), F32)],
        compiler_params=_params("arbitrary"),
    )(xr, gate, conv_state, h0, *weights)


def _out_proj_kernel(att_ref, rec_ref, x_ref, w_ref, g_ref, b_ref, o_ref):
    d_att = att_ref.shape[-1]
    mix = (jnp.dot(att_ref[...].astype(BF16), w_ref[0:d_att, :], preferred_element_type=F32)
           + jnp.dot(rec_ref[...].astype(BF16), w_ref[d_att:, :], preferred_element_type=F32))
    o_ref[...] = _layer_norm(ALPHA * x_ref[...] + mix, g_ref[...], b_ref[...])


def _out_proj(att, rec, x2d, w_bf16, g, b):
    m, d_model = x2d.shape
    tm = min(ROW_TILE, m)
    rows = lambda a: pl.BlockSpec((tm, a.shape[1]), lambda i: (i, 0))
    return pl.pallas_call(
        _out_proj_kernel,
        grid=(m // tm,),
        in_specs=[rows(att), rows(rec), rows(x2d), _resident(w_bf16.shape),
                  _resident(g.shape), _resident(b.shape)],
        out_specs=rows(x2d),
        out_shape=jax.ShapeDtypeStruct((m, d_model), F32),
        compiler_params=_params("arbitrary"),
    )(att, rec, x2d, w_bf16, g, b)


def _top_k_rows(vals, k):
    n = vals.shape[0]
    row = lax.broadcasted_iota(jnp.int32, vals.shape, 0)
    tv, ti = [], []
    for _ in range(k):
        m = jnp.max(vals, axis=0, keepdims=True)
        idx = jnp.min(jnp.where(vals == m, row, n), axis=0, keepdims=True)
        tv.append(m)
        ti.append(idx)
        vals = jnp.where(row == idx, -jnp.inf, vals)
    return jnp.concatenate(tv, axis=0), jnp.concatenate(ti, axis=0)


def _route_kernel(x_ref, wq_ref, keys_ref, eidx_ref, gate_ref):
    q = jnp.dot(x_ref[...].astype(BF16), wq_ref[...], preferred_element_type=F32)
    tokens = q.shape[0]
    e_rows, g_rows = [], []
    for h in range(PEER_HEADS):
        sv, si = [], []
        for p in range(2):
            c0 = (h * 2 + p) * HALF_KEY
            s = lax.dot_general(keys_ref[h, p].astype(BF16), q[:, c0:c0 + HALF_KEY].astype(BF16),
                                (((1,), (1,)), ((), ())), preferred_element_type=F32)
            v, i = _top_k_rows(s, TOPK)
            sv.append(v)
            si.append(i)
        widths = [TOPK // (a + 1) for a in range(TOPK)]
        cand = jnp.concatenate([sv[0][a:a + 1] + sv[1][0:w] for a, w in enumerate(widths)], axis=0)
        cand_idx = jnp.concatenate([si[0][a:a + 1] * N_KEYS + si[1][0:w]
                                    for a, w in enumerate(widths)], axis=0)
        pad = -cand.shape[0] % V7X_SUBLANES
        cand = jnp.concatenate([cand, jnp.full((pad, tokens), -jnp.inf, F32)], axis=0)
        cand_idx = jnp.concatenate([cand_idx, jnp.zeros((pad, tokens), jnp.int32)], axis=0)
        fv, fi = _top_k_rows(cand, TOPK)
        flat = lax.broadcasted_iota(jnp.int32, cand.shape, 0)
        eidx = jnp.concatenate(
            [jnp.sum(jnp.where(flat == fi[r:r + 1, :], cand_idx, 0), axis=0, keepdims=True)
             for r in range(TOPK)], axis=0)
        ex = jnp.exp(fv - fv[0:1, :])
        e_rows.append(eidx)
        g_rows.append(ex / jnp.sum(ex, axis=0, keepdims=True))
    eidx_ref[...] = jnp.concatenate(e_rows, axis=0).T
    gates = jnp.concatenate(g_rows, axis=0)
    for j in range(tokens // GATHER_TOKENS):
        gate_ref[j] = gates[:, j * GATHER_TOKENS:(j + 1) * GATHER_TOKENS]


def _route(x1, wq_bf16, sub_keys):
    n, d_model = x1.shape
    tm = min(ROUTE_TILE, n)
    groups = tm // GATHER_TOKENS
    return pl.pallas_call(
        _route_kernel,
        grid=(n // tm,),
        in_specs=[pl.BlockSpec((tm, d_model), lambda i: (i, 0)), _resident(wq_bf16.shape),
                  _resident(sub_keys.shape)],
        out_specs=[pl.BlockSpec((tm, N_PICKS), lambda i: (i, 0)),
                   pl.BlockSpec((groups, N_PICKS, GATHER_TOKENS), lambda i: (i, 0, 0))],
        out_shape=[jax.ShapeDtypeStruct((n, N_PICKS), jnp.int32),
                   jax.ShapeDtypeStruct((n // GATHER_TOKENS, N_PICKS, GATHER_TOKENS), F32)],
        compiler_params=_params("arbitrary"),
    )(x1, wq_bf16, sub_keys)


def _pack_kernel(u_ref, v_ref, o_ref):
    half = u_ref.shape[1] // 2

    def words(ref):
        bits = lax.bitcast_convert_type(ref[...].astype(BF16).astype(F32), jnp.uint32)
        return (bits[:, :half] >> 16) | (bits[:, half:] & jnp.uint32(0xFFFF0000))

    o_ref[...] = jnp.concatenate([words(u_ref), words(v_ref)], axis=1)[:, None, :]


def _pack_experts(expert_u, expert_v):
    n, d = expert_u.shape
    rows = min(ROW_TILE, n)
    blk = pl.BlockSpec((rows, d), lambda i: (i, 0))
    return pl.pallas_call(
        _pack_kernel,
        grid=(n // rows,),
        in_specs=[blk, blk],
        out_specs=pl.BlockSpec((rows, 1, d), lambda i: (i, 0, 0)),
        out_shape=jax.ShapeDtypeStruct((n, 1, d), jnp.uint32),
        compiler_params=_params("arbitrary"),
    )(expert_u, expert_v)


def _unpack_words(words):
    low = lax.bitcast_convert_type(words << 16, F32)
    high = lax.bitcast_convert_type(words & jnp.uint32(0xFFFF0000), F32)
    return low, high


def _expert_kernel(idx_ref, x_ref, gate_ref, g_ref, b_ref, tab_hbm, o_ref, *scratch, n_blocks):
    step = pl.program_id(0)
    tokens, d_model = x_ref.shape
    half = d_model // 2
    *bufs, sems = scratch
    n_slots = len(bufs)

    def slot_wait(slot):
        pltpu.make_async_copy(bufs[slot], bufs[slot], sems.at[slot]).wait()

    @pl.when(step == 0)
    def _zero_stand_ins():
        for buf in bufs[1:]:
            buf[...] = jnp.zeros_like(buf)

    for slot in range(n_slots):
        pl.when((step >= GATHER_LAG) & ((step - GATHER_LAG) % n_slots == slot))(
            functools.partial(slot_wait, slot))

    def gather_and_finish(slot):
        for t in range(tokens):
            for r in range(N_PICKS):
                pltpu.make_async_copy(tab_hbm.at[idx_ref[t, r]],
                                      bufs[slot].at[pl.ds(t * N_PICKS + r, 1)],
                                      sems.at[slot]).start(priority=r % 2)
        rows = bufs[(slot - GATHER_LAG) % n_slots]
        x = x_ref[...]
        hid_cols = []
        for t in range(tokens):
            u_low, u_high = _unpack_words(rows[t * N_PICKS:(t + 1) * N_PICKS, 0:half])
            prod = u_low * x[t:t + 1, 0:half] + u_high * x[t:t + 1, half:d_model]
            hid_cols.append(jnp.sum(prod, axis=-1, keepdims=True))
        w = gate_ref[0] * _gelu(jnp.concatenate(hid_cols, axis=1))
        outs = []
        for t in range(tokens):
            v_low, v_high = _unpack_words(rows[t * N_PICKS:(t + 1) * N_PICKS, half:d_model])
            wt = w[:, t:t + 1]
            outs.append(jnp.concatenate([jnp.sum(wt * v_low, axis=0, keepdims=True),
                                         jnp.sum(wt * v_high, axis=0, keepdims=True)], axis=1))
        ffn = jnp.concatenate(outs, axis=0)
        o_ref[...] = _layer_norm(ALPHA * x + ffn, g_ref[...], b_ref[...])

    for slot in range(n_slots):
        pl.when(step % n_slots == slot)(functools.partial(gather_and_finish, slot))

    @pl.when(step == n_blocks + GATHER_LAG - 1)
    def _drain():
        for lag in range(GATHER_LAG):
            slot_wait((n_blocks + lag) % n_slots)


def _experts(eidx, gates, x1, g, b, table):
    n, d_model = x1.shape
    tokens = GATHER_TOKENS
    n_blocks = n // tokens
    n_slots = GATHER_LAG + 1
    done = lambda i: jnp.maximum(i - GATHER_LAG, 0)
    tok_blk = lambda w: pl.BlockSpec((tokens, w), lambda i: (done(i), 0))
    return pl.pallas_call(
        functools.partial(_expert_kernel, n_blocks=n_blocks),
        grid=(n_blocks + GATHER_LAG,),
        in_specs=[pl.BlockSpec((tokens, N_PICKS), lambda i: (jnp.minimum(i, n_blocks - 1), 0),
                               memory_space=pltpu.SMEM),
                  tok_blk(d_model),
                  pl.BlockSpec((1, N_PICKS, tokens), lambda i: (done(i), 0, 0)),
                  _resident(g.shape), _resident(b.shape), pl.BlockSpec(memory_space=pl.ANY)],
        out_specs=tok_blk(d_model),
        out_shape=jax.ShapeDtypeStruct((n, d_model), F32),
        scratch_shapes=[pltpu.VMEM((tokens * N_PICKS, d_model), jnp.uint32)] * n_slots
                       + [pltpu.SemaphoreType.DMA((n_slots,))],
        compiler_params=_params("arbitrary"),
    )(eidx, x1, gates, g, b, table)


def _block_diag(w):
    n, c, d = w.shape
    eye = jnp.eye(n, dtype=w.dtype)
    return (eye[:, None, :, None] * w[:, :, None, :]).reshape(n * c, n * d)


def _prepare_weights(w_in, conv_w, conv_b, w_a, b_a, w_x, b_x, lam, w_out, ln1_g, ln1_b,
                     w_query, sub_keys, expert_u, expert_v, ln2_g, ln2_b):
    return (w_in.astype(BF16), conv_w, conv_b, _block_diag(w_a).astype(BF16), b_a,
            _block_diag(w_x).astype(BF16), b_x, lam, w_out.astype(BF16), ln1_g, ln1_b,
            w_query.astype(BF16), sub_keys, _pack_experts(expert_u, expert_v), ln2_g, ln2_b)


def _trunk_layer(x, cache, conv_state, h0, wts):
    (w_in, conv_w, conv_b, wa_bd, b_a, wx_bd, b_x, lam, w_out, ln1_g, ln1_b,
     w_query, sub_keys, expert_table, ln2_g, ln2_b) = wts
    b, t_len, d_model = x.shape
    d_lru = d_model - D_ATT
    x2d = x.reshape(b * t_len, d_model)
    slopes = 2.0 ** (-8.0 * jnp.arange(1, N_ATT_HEADS + 1, dtype=F32) / N_ATT_HEADS)

    head_major = lambda a: jnp.transpose(a, (0, 2, 1, 3))
    q, k, v, xr, gate = _in_proj(x2d, w_in, b, t_len, d_lru)
    seq = lambda a: a.reshape(b, t_len, a.shape[-1])
    if cache is None:
        att = _attn_prompt(slopes, q, k, v)
        keep = min(PATTERNS[-1][0], t_len)
        new_k, new_v = k[:, :, t_len - keep:], v[:, :, t_len - keep:]
    else:
        att, new_k, new_v = _attn_sample(slopes, q, k, v, head_major(cache[0]), head_major(cache[1]))
    row = lambda a: a.reshape(1, -1)
    rec, new_conv, h_last = _recurrent(
        seq(xr), seq(gate), conv_state, h0.reshape(b, 1, d_lru), conv_w, row(conv_b),
        wa_bd, row(b_a), wx_bd, row(b_x), row(lam))
    x1 = _out_proj(att.reshape(b * t_len, D_ATT), rec.reshape(b * t_len, d_lru), x2d,
                   w_out, row(ln1_g), row(ln1_b))
    eidx, gates2 = _route(x1, w_query, sub_keys)
    y = _experts(eidx, gates2, x1, row(ln2_g), row(ln2_b), expert_table)
    return (y.reshape(b, t_len, d_model), head_major(new_k), head_major(new_v), new_conv,
            h_last.reshape(b, d_lru))


def kernel(x_prompt, x_sample, cache_k, cache_v, state_conv, state_h, w_in, conv_w, conv_b, lru_w_a, lru_b_a, lru_w_x, lru_b_x, lru_lambda, w_out, ln1_g, ln1_b, peer_w_query, peer_sub_keys, peer_u, peer_v, ln2_g, ln2_b):
    yp, ys = x_prompt, x_sample
    outs_p, outs_s = [], []
    for layer in range(w_in.shape[0]):
        wts = _prepare_weights(*(w[layer] for w in (
            w_in, conv_w, conv_b, lru_w_a, lru_b_a, lru_w_x, lru_b_x, lru_lambda, w_out, ln1_g,
            ln1_b, peer_w_query, peer_sub_keys, peer_u, peer_v, ln2_g, ln2_b)))
        bp = yp.shape[0]
        d_lru = conv_w.shape[-1]
        yp, *rest_p = _trunk_layer(yp, None, jnp.zeros((bp, CONV_WIDTH - 1, d_lru), F32),
                                   jnp.zeros((bp, d_lru), F32), wts)
        ys, *rest_s = _trunk_layer(ys, (cache_k[layer], cache_v[layer]), state_conv[layer],
                                   state_h[layer], wts)
        outs_p.append(rest_p)
        outs_s.append(rest_s)
    stack = lambda outs, j: jnp.stack([o[j] for o in outs])
    return (yp, ys, *(stack(outs_p, j) for j in range(4)), *(stack(outs_s, j) for j in range(4)))
```

```python
import functools

import jax
import jax.numpy as jnp
from jax import lax
from jax.experimental import pallas as pl
from jax.experimental.pallas import tpu as pltpu

F32 = jnp.float32
BF16 = jnp.bfloat16

N_ATT_HEADS = 12
HEAD_DIM = 128
D_ATT = N_ATT_HEADS * HEAD_DIM
ATT_SCALE = HEAD_DIM ** -0.5
PATTERNS = ((128, 1), (512, 4), (2048, 16))
N_LRU_BLOCKS = 8
CONV_WIDTH = 4
LRU_C = 8.0
N_KEYS = 128
PEER_HEADS = 8
HALF_KEY = 128
TOPK = 16
DEPTH = 1
ALPHA = (2.0 * DEPTH) ** 0.25
LN_EPS = 1e-5

V7X_SUBLANES = 8
V7X_LANES = 128
V7X_VMEM_LIMIT_BYTES = 56 * 1024 * 1024

ROW_TILE = 256
ATT_BLOCK = 128
ROUTE_TILE = 128
GATHER_TOKENS = 16
GATHER_LAG = 2
N_PICKS = PEER_HEADS * TOPK


def _params(*sem):
    return pltpu.CompilerParams(dimension_semantics=sem, vmem_limit_bytes=V7X_VMEM_LIMIT_BYTES)


def _resident(shape):
    return pl.BlockSpec(shape, lambda *_: (0,) * len(shape), pipeline_mode=pl.Buffered(1))


def _layer_norm(z, g, b):
    mu = jnp.mean(z, axis=-1, keepdims=True)
    zc = z - mu
    var = jnp.mean(zc * zc, axis=-1, keepdims=True)
    return zc * lax.rsqrt(var + LN_EPS) * g + b


def _gelu(x):
    return 0.5 * x * (1.0 + lax.erf(x * (2.0 ** -0.5)))


def _in_proj_kernel(x_ref, w_ref, q_ref, k_ref, v_ref, xr_ref, gate_ref):
    xb = x_ref[...].astype(BF16)
    col = 0
    for o_ref in (q_ref, k_ref, v_ref):
        res = jnp.dot(xb, w_ref[:, col:col + D_ATT], preferred_element_type=F32)
        nb, _, tr, _ = o_ref.shape
        for bb in range(nb):
            for h in range(N_ATT_HEADS):
                o_ref[bb, h] = res[bb * tr:(bb + 1) * tr, h * HEAD_DIM:(h + 1) * HEAD_DIM]
        col += D_ATT
    for o_ref in (xr_ref, gate_ref):
        width = o_ref.shape[-1]
        o_ref[...] = jnp.dot(xb, w_ref[:, col:col + width], preferred_element_type=F32)
        col += width


def _in_proj(x2d, w_bf16, batch, t_len, d_lru):
    m, d_model = x2d.shape
    tm = min(ROW_TILE, m)
    tr = min(tm, t_len)
    nb = tm // tr
    tiles_per_batch = t_len // tr
    head_blk = pl.BlockSpec((nb, N_ATT_HEADS, tr, HEAD_DIM),
                            lambda i: (i // tiles_per_batch, 0, i % tiles_per_batch, 0))
    head_shape = jax.ShapeDtypeStruct((batch, N_ATT_HEADS, t_len, HEAD_DIM), F32)
    return pl.pallas_call(
        _in_proj_kernel,
        grid=(m // tm,),
        in_specs=[pl.BlockSpec((tm, d_model), lambda i: (i, 0)), _resident(w_bf16.shape)],
        out_specs=[head_blk] * 3 + [pl.BlockSpec((tm, d_lru), lambda i: (i, 0))] * 2,
        out_shape=[head_shape] * 3 + [jax.ShapeDtypeStruct((m, d_lru), F32)] * 2,
        compiler_params=_params("arbitrary"),
    )(x2d, w_bf16)


def _pattern_softmax(sb, delta, window, dil):
    valid = (delta >= 0) & (delta <= window) & ((delta & (dil - 1)) == 0)
    sp = jnp.where(valid, sb, -jnp.inf)
    m = jnp.max(sp, axis=-1, keepdims=True)
    return sp, m


def _merge_patterns(ms, ls, os_):
    m_max = functools.reduce(jnp.maximum, ms)
    ws = [jnp.exp(m - m_max) for m in ms]
    num = sum(w * o for w, o in zip(ws, os_))
    den = sum(w * l for w, l in zip(ws, ls))
    return num / den


def _qk(q, k):
    return lax.dot_general(q.astype(BF16), k.astype(BF16), (((1,), (1,)), ((), ())),
                           preferred_element_type=F32) * ATT_SCALE


def _attn_prompt_kernel(slopes_ref, q_ref, k_ref, v_ref, o_ref, m_scr, l_scr, acc_scr):
    seq = q_ref.shape[2]
    slope = slopes_ref[pl.program_id(1)]
    for p, (window, dil) in enumerate(PATTERNS):
        n_class = seq // dil
        reach = window // dil
        blk = min(ATT_BLOCK, n_class)
        for r in range(dil):
            def class_rows(ref, start, size):
                return ref[0, 0, pl.ds(r + dil * start, size, stride=dil), :] if dil > 1 else \
                    ref[0, 0, start:start + size, :]
            for c in range(n_class // blk):
                k0 = max(0, c * blk - reach)
                n_keys = (c + 1) * blk - k0
                s = _qk(class_rows(q_ref, c * blk, blk), class_rows(k_ref, k0, n_keys))
                dist = (c * blk - k0 + lax.broadcasted_iota(jnp.int32, (blk, n_keys), 0)
                        - lax.broadcasted_iota(jnp.int32, (blk, n_keys), 1))
                sp = jnp.where((dist >= 0) & (dist <= reach),
                               s - (slope * dil) * dist.astype(F32), -jnp.inf)
                m = jnp.max(sp, axis=-1, keepdims=True)
                prob = jnp.exp(sp - m)
                out = jnp.dot(prob.astype(BF16), class_rows(v_ref, k0, n_keys).astype(BF16),
                              preferred_element_type=F32)
                dst = pl.ds(r + dil * c * blk, blk, stride=dil) if dil > 1 else \
                    pl.ds(c * blk, blk)
                m_scr[p, dst, :] = jnp.broadcast_to(m, out.shape)
                l_scr[p, dst, :] = jnp.broadcast_to(jnp.sum(prob, axis=-1, keepdims=True), out.shape)
                acc_scr[p, dst, :] = out
    n_pat = len(PATTERNS)
    o_ref[0] = _merge_patterns([m_scr[p] for p in range(n_pat)], [l_scr[p] for p in range(n_pat)],
                               [acc_scr[p] for p in range(n_pat)])


def _attn_prompt(slopes, q, k, v):
    b, _, seq, _ = q.shape
    assert all(seq // d <= ATT_BLOCK or w // d <= ATT_BLOCK for w, d in PATTERNS)
    blk = pl.BlockSpec((1, 1, seq, HEAD_DIM), lambda bi, h, *_: (bi, h, 0, 0))
    stats = pltpu.VMEM((len(PATTERNS), seq, HEAD_DIM), F32)
    return pl.pallas_call(
        _attn_prompt_kernel,
        grid_spec=pltpu.PrefetchScalarGridSpec(
            num_scalar_prefetch=1, grid=(b, N_ATT_HEADS),
            in_specs=[blk, blk, blk],
            out_specs=pl.BlockSpec((1, seq, HEAD_DIM), lambda bi, h, *_: (bi, 0, h)),
            scratch_shapes=[stats, stats, stats]),
        out_shape=jax.ShapeDtypeStruct((b, seq, D_ATT), F32),
        compiler_params=_params("arbitrary", "arbitrary"),
    )(slopes, q, k, v)


def _attn_sample_kernel(slopes_ref, qn_ref, kn_ref, vn_ref, ck_ref, cv_ref,
                        att_ref, ok_ref, ov_ref, *, heads_per_step):
    t_new = qn_ref.shape[2]
    n_past = ck_ref.shape[2]
    for c_ref, n_ref, o_ref in ((ck_ref, kn_ref, ok_ref), (cv_ref, vn_ref, ov_ref)):
        o_ref[0, :, 0:n_past - t_new, :] = c_ref[0, :, t_new:n_past, :]
        o_ref[0, :, n_past - t_new:n_past, :] = n_ref[0]
    t_row = lax.broadcasted_iota(jnp.int32, (t_new, n_past), 0)
    d_old = n_past + t_row - lax.broadcasted_iota(jnp.int32, (t_new, n_past), 1)
    d_new = (lax.broadcasted_iota(jnp.int32, (t_new, t_new), 0)
             - lax.broadcasted_iota(jnp.int32, (t_new, t_new), 1))
    for hh in range(heads_per_step):
        slope = slopes_ref[pl.program_id(1) * heads_per_step + hh]
        cols = slice(hh * HEAD_DIM, (hh + 1) * HEAD_DIM)
        q = qn_ref[0, hh]
        sb_old = _qk(q, ck_ref[0, hh]) - slope * d_old.astype(F32)
        sb_new = _qk(q, kn_ref[0, hh]) - slope * d_new.astype(F32)
        v_old = cv_ref[0, hh].astype(BF16)
        v_new = vn_ref[0, hh].astype(BF16)
        ms, ls, os_ = [], [], []
        for window, dil in PATTERNS:
            sp_old, m_old = _pattern_softmax(sb_old, d_old, window, dil)
            sp_new, m_new = _pattern_softmax(sb_new, d_new, window, dil)
            m = jnp.maximum(m_old, m_new)
            p_old = jnp.exp(sp_old - m)
            p_new = jnp.exp(sp_new - m)
            ms.append(m)
            ls.append(jnp.sum(p_old, axis=-1, keepdims=True) + jnp.sum(p_new, axis=-1, keepdims=True))
            os_.append(jnp.dot(p_old.astype(BF16), v_old, preferred_element_type=F32)
                       + jnp.dot(p_new.astype(BF16), v_new, preferred_element_type=F32))
        att_ref[0, :, cols] = _merge_patterns(ms, ls, os_)


def _attn_sample(slopes, qn, kn, vn, cache_k, cache_v):
    b, _, t_new, _ = qn.shape
    n_past = cache_k.shape[2]
    hps = 2
    new_blk = pl.BlockSpec((1, hps, t_new, HEAD_DIM), lambda bi, j, *_: (bi, j, 0, 0))
    buf_blk = pl.BlockSpec((1, hps, n_past, HEAD_DIM), lambda bi, j, *_: (bi, j, 0, 0))
    att_blk = pl.BlockSpec((1, t_new, hps * HEAD_DIM), lambda bi, j, *_: (bi, 0, j))
    return pl.pallas_call(
        functools.partial(_attn_sample_kernel, heads_per_step=hps),
        grid_spec=pltpu.PrefetchScalarGridSpec(
            num_scalar_prefetch=1, grid=(b, N_ATT_HEADS // hps),
            in_specs=[new_blk, new_blk, new_blk, buf_blk, buf_blk],
            out_specs=[att_blk, buf_blk, buf_blk]),
        out_shape=[jax.ShapeDtypeStruct((b, t_new, D_ATT), F32),
                   jax.ShapeDtypeStruct(cache_k.shape, F32),
                   jax.ShapeDtypeStruct(cache_v.shape, F32)],
        compiler_params=_params("arbitrary", "arbitrary"),
    )(slopes, qn, kn, vn, cache_k, cache_v)


def _recurrent_kernel(xr_ref, gate_ref, cs_ref, h0_ref, cw_ref, cb_ref, wa_ref, ba_ref,
                      wx_ref, bx_ref, lam_ref, rec_ref, nc_ref, hl_ref,
                      xext_ref, a_ref, u_ref, *, chunk):
    t_len = xr_ref.shape[1]
    pad = V7X_SUBLANES
    hist = CONV_WIDTH - 1
    xext_ref[pad - hist:pad, :] = cs_ref[0]
    xext_ref[pad:pad + t_len, :] = xr_ref[0]
    nc_ref[0] = xext_ref[pad + t_len - hist:pad + t_len, :]
    z = -lam_ref[...]
    softplus = jnp.maximum(z, 0.0) + jnp.log(1.0 + jnp.exp(-jnp.abs(z)))
    for c in range(t_len // chunk):
        r0 = c * chunk
        xc = cb_ref[...]
        for j in range(CONV_WIDTH):
            xc = xc + xext_ref[pad - hist + j + r0:pad - hist + j + r0 + chunk, :] * cw_ref[j:j + 1, :]
        xcb = xc.astype(BF16)
        r = jax.nn.sigmoid(jnp.dot(xcb, wa_ref[...], preferred_element_type=F32) + ba_ref[...])
        i = jax.nn.sigmoid(jnp.dot(xcb, wx_ref[...], preferred_element_type=F32) + bx_ref[...])
        log_a = -LRU_C * r * softplus
        a = jnp.exp(log_a)
        a_ref[r0:r0 + chunk, :] = a
        u_ref[r0:r0 + chunk, :] = jnp.sqrt(1.0 - jnp.exp(2.0 * log_a)) * (i * xc)

    rows = V7X_SUBLANES

    def step(tile, h):
        base = pl.multiple_of(tile * rows, rows)
        a = a_ref[pl.ds(base, rows), :]
        u = u_ref[pl.ds(base, rows), :]
        out = []
        for s in range(rows):
            h = a[s:s + 1, :] * h + u[s:s + 1, :]
            out.append(h)
        a_ref[pl.ds(base, rows), :] = jnp.concatenate(out, axis=0)
        return h

    h_last = lax.fori_loop(0, t_len // rows, step, h0_ref[0])
    hl_ref[0] = h_last
    for c in range(t_len // chunk):
        r0 = c * chunk
        rec_ref[0, r0:r0 + chunk, :] = a_ref[r0:r0 + chunk, :] * _gelu(gate_ref[0, r0:r0 + chunk, :])


def _recurrent(xr, gate, conv_state, h0, conv_w, conv_b, wa_bd, b_a, wx_bd, b_x, lam):
    b, t_len, c = xr.shape
    chunk = min(ROW_TILE, t_len)
    hist = CONV_WIDTH - 1
    seq_blk = pl.BlockSpec((1, t_len, c), lambda bi: (bi, 0, 0))
    row = lambda n: pl.BlockSpec((1, n, c), lambda bi: (bi, 0, 0))
    vec = lambda a: pl.BlockSpec(a.shape, lambda bi: (0,) * a.ndim)
    weights = (conv_w, conv_b, wa_bd, b_a, wx_bd, b_x, lam)
    return pl.pallas_call(
        functools.partial(_recurrent_kernel, chunk=chunk),
        grid=(b,),
        in_specs=[seq_blk, seq_blk, row(hist), row(1)] + [vec(w) for w in weights],
        out_specs=[seq_blk, row(hist), row(1)],
        out_shape=[jax.ShapeDtypeStruct((b, t_len, c), F32),
                   jax.ShapeDtypeStruct((b, hist, c), F32),
                   jax.ShapeDtypeStruct((b, 1, c), F32)],
        scratch_shapes=[pltpu.VMEM((V7X_SUBLANES + t_len, c), F32),
                        pltpu.VMEM((t_len, c), F32), pltpu.VMEM((t_len, c), F32)],
        compiler_params=_params("arbitrary"),
    )(xr, gate, conv_state, h0, *weights)


def _out_proj_kernel(att_ref, rec_ref, x_ref, w_ref, g_ref, b_ref, o_ref):
    d_att = att_ref.shape[-1]
    mix = (jnp.dot(att_ref[...].astype(BF16), w_ref[0:d_att, :], preferred_element_type=F32)
           + jnp.dot(rec_ref[...].astype(BF16), w_ref[d_att:, :], preferred_element_type=F32))
    o_ref[...] = _layer_norm(ALPHA * x_ref[...] + mix, g_ref[...], b_ref[...])


def _out_proj(att, rec, x2d, w_bf16, g, b):
    m, d_model = x2d.shape
    tm = min(ROW_TILE, m)
    rows = lambda a: pl.BlockSpec((tm, a.shape[1]), lambda i: (i, 0))
    return pl.pallas_call(
        _out_proj_kernel,
        grid=(m // tm,),
        in_specs=[rows(att), rows(rec), rows(x2d), _resident(w_bf16.shape),
                  _resident(g.shape), _resident(b.shape)],
        out_specs=rows(x2d),
        out_shape=jax.ShapeDtypeStruct((m, d_model), F32),
        compiler_params=_params("arbitrary"),
    )(att, rec, x2d, w_bf16, g, b)


def _top_k_rows(vals, k):
    n = vals.shape[0]
    row = lax.broadcasted_iota(jnp.int32, vals.shape, 0)
    tv, ti = [], []
    for _ in range(k):
        m = jnp.max(vals, axis=0, keepdims=True)
        idx = jnp.min(jnp.where(vals == m, row, n), axis=0, keepdims=True)
        tv.append(m)
        ti.append(idx)
        vals = jnp.where(row == idx, -jnp.inf, vals)
    return jnp.concatenate(tv, axis=0), jnp.concatenate(ti, axis=0)


def _route_kernel(x_ref, wq_ref, keys_ref, eidx_ref, gate_ref):
    q = jnp.dot(x_ref[...].astype(BF16), wq_ref[...], preferred_element_type=F32)
    tokens = q.shape[0]
    e_rows, g_rows = [], []
    for h in range(PEER_HEADS):
        sv, si = [], []
        for p in range(2):
            c0 = (h * 2 + p) * HALF_KEY
            s = lax.dot_general(keys_ref[h, p].astype(BF16), q[:, c0:c0 + HALF_KEY].astype(BF16),
                                (((1,), (1,)), ((), ())), preferred_element_type=F32)
            v, i = _top_k_rows(s, TOPK)
            sv.append(v)
            si.append(i)
        widths = [TOPK // (a + 1) for a in range(TOPK)]
        cand = jnp.concatenate([sv[0][a:a + 1] + sv[1][0:w] for a, w in enumerate(widths)], axis=0)
        cand_idx = jnp.concatenate([si[0][a:a + 1] * N_KEYS + si[1][0:w]
                                    for a, w in enumerate(widths)], axis=0)
        pad = -cand.shape[0] % V7X_SUBLANES
        cand = jnp.concatenate([cand, jnp.full((pad, tokens), -jnp.inf, F32)], axis=0)
        cand_idx = jnp.concatenate([cand_idx, jnp.zeros((pad, tokens), jnp.int32)], axis=0)
        fv, fi = _top_k_rows(cand, TOPK)
        flat = lax.broadcasted_iota(jnp.int32, cand.shape, 0)
        eidx = jnp.concatenate(
            [jnp.sum(jnp.where(flat == fi[r:r + 1, :], cand_idx, 0), axis=0, keepdims=True)
             for r in range(TOPK)], axis=0)
        ex = jnp.exp(fv - fv[0:1, :])
        e_rows.append(eidx)
        g_rows.append(ex / jnp.sum(ex, axis=0, keepdims=True))
    eidx_ref[...] = jnp.concatenate(e_rows, axis=0).T
    gates = jnp.concatenate(g_rows, axis=0)
    for j in range(tokens // GATHER_TOKENS):
        gate_ref[j] = gates[:, j * GATHER_TOKENS:(j + 1) * GATHER_TOKENS]


def _route(x1, wq_bf16, sub_keys):
    n, d_model = x1.shape
    tm = min(ROUTE_TILE, n)
    groups = tm // GATHER_TOKENS
    return pl.pallas_call(
        _route_kernel,
        grid=(n // tm,),
        in_specs=[pl.BlockSpec((tm, d_model), lambda i: (i, 0)), _resident(wq_bf16.shape),
                  _resident(sub_keys.shape)],
        out_specs=[pl.BlockSpec((tm, N_PICKS), lambda i: (i, 0)),
                   pl.BlockSpec((groups, N_PICKS, GATHER_TOKENS), lambda i: (i, 0, 0))],
        out_shape=[jax.ShapeDtypeStruct((n, N_PICKS), jnp.int32),
                   jax.ShapeDtypeStruct((n // GATHER_TOKENS, N_PICKS, GATHER_TOKENS), F32)],
        compiler_params=_params("arbitrary"),
    )(x1, wq_bf16, sub_keys)


def _pack_kernel(u_ref, v_ref, o_ref):
    half = u_ref.shape[1] // 2

    def words(ref):
        bits = lax.bitcast_convert_type(ref[...].astype(BF16).astype(F32), jnp.uint32)
        return (bits[:, :half] >> 16) | (bits[:, half:] & jnp.uint32(0xFFFF0000))

    o_ref[...] = jnp.concatenate([words(u_ref), words(v_ref)], axis=1)[:, None, :]


def _pack_experts(expert_u, expert_v):
    n, d = expert_u.shape
    rows = min(ROW_TILE, n)
    blk = pl.BlockSpec((rows, d), lambda i: (i, 0))
    return pl.pallas_call(
        _pack_kernel,
        grid=(n // rows,),
        in_specs=[blk, blk],
        out_specs=pl.BlockSpec((rows, 1, d), lambda i: (i, 0, 0)),
        out_shape=jax.ShapeDtypeStruct((n, 1, d), jnp.uint32),
        compiler_params=_params("arbitrary"),
    )(expert_u, expert_v)


def _unpack_words(words):
    low = lax.bitcast_convert_type(words << 16, F32)
    high = lax.bitcast_convert_type(words & jnp.uint32(0xFFFF0000), F32)
    return low, high


def _expert_kernel(idx_ref, x_ref, gate_ref, g_ref, b_ref, tab_hbm, o_ref, *scratch, n_blocks):
    step = pl.program_id(0)
    tokens, d_model = x_ref.shape
    half = d_model // 2
    *bufs, sems = scratch
    n_slots = len(bufs)

    def slot_wait(slot):
        pltpu.make_async_copy(bufs[slot], bufs[slot], sems.at[slot]).wait()

    @pl.when(step == 0)
    def _zero_stand_ins():
        for buf in bufs[1:]:
            buf[...] = jnp.zeros_like(buf)

    for slot in range(n_slots):
        pl.when((step >= GATHER_LAG) & ((step - GATHER_LAG) % n_slots == slot))(
            functools.partial(slot_wait, slot))

    def gather_and_finish(slot):
        for t in range(tokens):
            for r in range(N_PICKS):
                pltpu.make_async_copy(tab_hbm.at[idx_ref[t, r]],
                                      bufs[slot].at[pl.ds(t * N_PICKS + r, 1)],
                                      sems.at[slot]).start(priority=r % 2)
        rows = bufs[(slot - GATHER_LAG) % n_slots]
        x = x_ref[...]
        hid_cols = []
        for t in range(tokens):
            u_low, u_high = _unpack_words(rows[t * N_PICKS:(t + 1) * N_PICKS, 0:half])
            prod = u_low * x[t:t + 1, 0:half] + u_high * x[t:t + 1, half:d_model]
            hid_cols.append(jnp.sum(prod, axis=-1, keepdims=True))
        w = gate_ref[0] * _gelu(jnp.concatenate(hid_cols, axis=1))
        outs = []
        for t in range(tokens):
            v_low, v_high = _unpack_words(rows[t * N_PICKS:(t + 1) * N_PICKS, half:d_model])
            wt = w[:, t:t + 1]
            outs.append(jnp.concatenate([jnp.sum(wt * v_low, axis=0, keepdims=True),
                                         jnp.sum(wt * v_high, axis=0, keepdims=True)], axis=1))
        ffn = jnp.concatenate(outs, axis=0)
        o_ref[...] = _layer_norm(ALPHA * x + ffn, g_ref[...], b_ref[...])

    for slot in range(n_slots):
        pl.when(step % n_slots == slot)(functools.partial(gather_and_finish, slot))

    @pl.when(step == n_blocks + GATHER_LAG - 1)
    def _drain():
        for lag in range(GATHER_LAG):
            slot_wait((n_blocks + lag) % n_slots)


def _experts(eidx, gates, x1, g, b, table):
    n, d_model = x1.shape
    tokens = GATHER_TOKENS
    n_blocks = n // tokens
    n_slots = GATHER_LAG + 1
    done = lambda i: jnp.maximum(i - GATHER_LAG, 0)
    tok_blk = lambda w: pl.BlockSpec((tokens, w), lambda i: (done(i), 0))
    return pl.pallas_call(
        functools.partial(_expert_kernel, n_blocks=n_blocks),
        grid=(n_blocks + GATHER_LAG,),
        in_specs=[pl.BlockSpec((tokens, N_PICKS), lambda i: (jnp.minimum(i, n_blocks - 1), 0),
                               memory_space=pltpu.SMEM),
                  tok_blk(d_model),
                  pl.BlockSpec((1, N_PICKS, tokens), lambda i: (done(i), 0, 0)),
                  _resident(g.shape), _resident(b.shape), pl.BlockSpec(memory_space=pl.ANY)],
        out_specs=tok_blk(d_model),
        out_shape=jax.ShapeDtypeStruct((n, d_model), F32),
        scratch_shapes=[pltpu.VMEM((tokens * N_PICKS, d_model), jnp.uint32)] * n_slots
                       + [pltpu.SemaphoreType.DMA((n_slots,))],
        compiler_params=_params("arbitrary"),
    )(eidx, x1, gates, g, b, table)


def _block_diag(w):
    n, c, d = w.shape
    eye = jnp.eye(n, dtype=w.dtype)
    return (eye[:, None, :, None] * w[:, :, None, :]).reshape(n * c, n * d)


def _prepare_weights(w_in, conv_w, conv_b, w_a, b_a, w_x, b_x, lam, w_out, ln1_g, ln1_b,
                     w_query, sub_keys, expert_u, expert_v, ln2_g, ln2_b):
    return (w_in.astype(BF16), conv_w, conv_b, _block_diag(w_a).astype(BF16), b_a,
            _block_diag(w_x).astype(BF16), b_x, lam, w_out.astype(BF16), ln1_g, ln1_b,
            w_query.astype(BF16), sub_keys, _pack_experts(expert_u, expert_v), ln2_g, ln2_b)


def _trunk_layer(x, cache, conv_state, h0, wts):
    (w_in, conv_w, conv_b, wa_bd, b_a, wx_bd, b_x, lam, w_out, ln1_g, ln1_b,
     w_query, sub_keys, expert_table, ln2_g, ln2_b) = wts
    b, t_len, d_model = x.shape
    d_lru = d_model - D_ATT
    x2d = x.reshape(b * t_len, d_model)
    slopes = 2.0 ** (-8.0 * jnp.arange(1, N_ATT_HEADS + 1, dtype=F32) / N_ATT_HEADS)

    head_major = lambda a: jnp.transpose(a, (0, 2, 1, 3))
    q, k, v, xr, gate = _in_proj(x2d, w_in, b, t_len, d_lru)
    seq = lambda a: a.reshape(b, t_len, a.shape[-1])
    if cache is None:
        att = _attn_prompt(slopes, q, k, v)
        keep = min(PATTERNS[-1][0], t_len)
        new_k, new_v = k[:, :, t_len - keep:], v[:, :, t_len - keep:]
    else:
        att, new_k, new_v = _attn_sample(slopes, q, k, v, head_major(cache[0]), head_major(cache[1]))
    row = lambda a: a.reshape(1, -1)
    rec, new_conv, h_last = _recurrent(
        seq(xr), seq(gate), conv_state, h0.reshape(b, 1, d_lru), conv_w, row(conv_b),
        wa_bd, row(b_a), wx_bd, row(b_x), row(lam))
    x1 = _out_proj(att.reshape(b * t_len, D_ATT), rec.reshape(b * t_len, d_lru), x2d,
                   w_out, row(ln1_g), row(ln1_b))
    eidx, gates2 = _route(x1, w_query, sub_keys)
    y = _experts(eidx, gates2, x1, row(ln2_g), row(ln2_b), expert_table)
    return (y.reshape(b, t_len, d_model), head_major(new_k), head_major(new_v), new_conv,
            h_last.reshape(b, d_lru))


def kernel(x_prompt, x_sample, cache_k, cache_v, state_conv, state_h, w_in, conv_w, conv_b, lru_w_a, lru_b_a, lru_w_x, lru_b_x, lru_lambda, w_out, ln1_g, ln1_b, peer_w_query, peer_sub_keys, peer_u, peer_v, ln2_g, ln2_b):
    yp, ys = x_prompt, x_sample
    outs_p, outs_s = [], []
    for layer in range(w_in.shape[0]):
        wts = _prepare_weights(*(w[layer] for w in (
            w_in, conv_w, conv_b, lru_w_a, lru_b_a, lru_w_x, lru_b_x, lru_lambda, w_out, ln1_g,
            ln1_b, peer_w_query, peer_sub_keys, peer_u, peer_v, ln2_g, ln2_b)))
        bp = yp.shape[0]
        d_lru = conv_w.shape[-1]
        yp, *rest_p = _trunk_layer(yp, None, jnp.zeros((bp, CONV_WIDTH - 1, d_lru), F32),
                                   jnp.zeros((bp, d_lru), F32), wts)
        ys, *rest_s = _trunk_layer(ys, (cache_k[layer], cache_v[layer]), state_conv[layer],
                                   state_h[layer], wts)
        outs_p.append(rest_p)
        outs_s.append(rest_s)
    stack = lambda outs, j: jnp.stack([o[j] for o in outs])
    return (yp, ys, *(stack(outs_p, j) for j in range(4)), *(stack(outs_s, j) for j in range(4)))
```

```python
import functools

import jax
import jax.numpy as jnp
from jax import lax
from jax.experimental import pallas as pl
from jax.experimental.pallas import tpu as pltpu

F32 = jnp.float32
BF16 = jnp.bfloat16

N_ATT_HEADS = 12
HEAD_DIM = 128
D_ATT = N_ATT_HEADS * HEAD_DIM
ATT_SCALE = HEAD_DIM ** -0.5
PATTERNS = ((128, 1), (512, 4), (2048, 16))
N_LRU_BLOCKS = 8
CONV_WIDTH = 4
LRU_C = 8.0
N_KEYS = 128
PEER_HEADS = 8
HALF_KEY = 128
TOPK = 16
DEPTH = 1
ALPHA = (2.0 * DEPTH) ** 0.25
LN_EPS = 1e-5

V7X_SUBLANES = 8
V7X_LANES = 128
V7X_VMEM_LIMIT_BYTES = 56 * 1024 * 1024

ROW_TILE = 256
ATT_BLOCK = 128
ROUTE_TILE = 128
GATHER_TOKENS = 8
ROW_PITCH = 2048 // V7X_LANES + 1
GATHER_LAG = 2
N_PICKS = PEER_HEADS * TOPK


def _params(*sem):
    return pltpu.CompilerParams(dimension_semantics=sem, vmem_limit_bytes=V7X_VMEM_LIMIT_BYTES)


def _resident(shape):
    return pl.BlockSpec(shape, lambda *_: (0,) * len(shape), pipeline_mode=pl.Buffered(1))


def _layer_norm(z, g, b):
    mu = jnp.mean(z, axis=-1, keepdims=True)
    zc = z - mu
    var = jnp.mean(zc * zc, axis=-1, keepdims=True)
    return zc * lax.rsqrt(var + LN_EPS) * g + b


def _gelu(x):
    return 0.5 * x * (1.0 + lax.erf(x * (2.0 ** -0.5)))


def _in_proj_kernel(x_ref, w_ref, q_ref, k_ref, v_ref, xr_ref, gate_ref):
    xb = x_ref[...].astype(BF16)
    col = 0
    for o_ref in (q_ref, k_ref, v_ref):
        res = jnp.dot(xb, w_ref[:, col:col + D_ATT], preferred_element_type=F32)
        nb, _, tr, _ = o_ref.shape
        for bb in range(nb):
            for h in range(N_ATT_HEADS):
                o_ref[bb, h] = res[bb * tr:(bb + 1) * tr, h * HEAD_DIM:(h + 1) * HEAD_DIM]
        col += D_ATT
    for o_ref in (xr_ref, gate_ref):
        width = o_ref.shape[-1]
        o_ref[...] = jnp.dot(xb, w_ref[:, col:col + width], preferred_element_type=F32)
        col += width


def _in_proj(x2d, w_bf16, batch, t_len, d_lru):
    m, d_model = x2d.shape
    tm = min(ROW_TILE, m)
    tr = min(tm, t_len)
    nb = tm // tr
    tiles_per_batch = t_len // tr
    head_blk = pl.BlockSpec((nb, N_ATT_HEADS, tr, HEAD_DIM),
                            lambda i: (i // tiles_per_batch, 0, i % tiles_per_batch, 0))
    head_shape = jax.ShapeDtypeStruct((batch, N_ATT_HEADS, t_len, HEAD_DIM), F32)
    return pl.pallas_call(
        _in_proj_kernel,
        grid=(m // tm,),
        in_specs=[pl.BlockSpec((tm, d_model), lambda i: (i, 0)), _resident(w_bf16.shape)],
        out_specs=[head_blk] * 3 + [pl.BlockSpec((tm, d_lru), lambda i: (i, 0))] * 2,
        out_shape=[head_shape] * 3 + [jax.ShapeDtypeStruct((m, d_lru), F32)] * 2,
        compiler_params=_params("arbitrary"),
    )(x2d, w_bf16)


def _pattern_softmax(sb, delta, window, dil):
    valid = (delta >= 0) & (delta <= window) & ((delta & (dil - 1)) == 0)
    sp = jnp.where(valid, sb, -jnp.inf)
    m = jnp.max(sp, axis=-1, keepdims=True)
    return sp, m


def _merge_patterns(ms, ls, os_):
    m_max = functools.reduce(jnp.maximum, ms)
    ws = [jnp.exp(m - m_max) for m in ms]
    num = sum(w * o for w, o in zip(ws, os_))
    den = sum(w * l for w, l in zip(ws, ls))
    return num / den


def _qk(q, k):
    return lax.dot_general(q.astype(BF16), k.astype(BF16), (((1,), (1,)), ((), ())),
                           preferred_element_type=F32) * ATT_SCALE


def _attn_prompt_kernel(slopes_ref, q_ref, k_ref, v_ref, o_ref, m_scr, l_scr, acc_scr):
    seq = q_ref.shape[2]
    slope = slopes_ref[pl.program_id(1)]
    for p, (window, dil) in enumerate(PATTERNS):
        n_class = seq // dil
        reach = window // dil
        blk = min(ATT_BLOCK, n_class)
        for r in range(dil):
            def class_rows(ref, start, size):
                return ref[0, 0, pl.ds(r + dil * start, size, stride=dil), :] if dil > 1 else \
                    ref[0, 0, start:start + size, :]
            for c in range(n_class // blk):
                k0 = max(0, c * blk - reach)
                n_keys = (c + 1) * blk - k0
                s = _qk(class_rows(q_ref, c * blk, blk), class_rows(k_ref, k0, n_keys))
                dist = (c * blk - k0 + lax.broadcasted_iota(jnp.int32, (blk, n_keys), 0)
                        - lax.broadcasted_iota(jnp.int32, (blk, n_keys), 1))
                sp = jnp.where((dist >= 0) & (dist <= reach),
                               s - (slope * dil) * dist.astype(F32), -jnp.inf)
                m = jnp.max(sp, axis=-1, keepdims=True)
                prob = jnp.exp(sp - m)
                out = jnp.dot(prob.astype(BF16), class_rows(v_ref, k0, n_keys).astype(BF16),
                              preferred_element_type=F32)
                dst = pl.ds(r + dil * c * blk, blk, stride=dil) if dil > 1 else \
                    pl.ds(c * blk, blk)
                m_scr[p, dst, :] = jnp.broadcast_to(m, out.shape)
                l_scr[p, dst, :] = jnp.broadcast_to(jnp.sum(prob, axis=-1, keepdims=True), out.shape)
                acc_scr[p, dst, :] = out
    n_pat = len(PATTERNS)
    o_ref[0] = _merge_patterns([m_scr[p] for p in range(n_pat)], [l_scr[p] for p in range(n_pat)],
                               [acc_scr[p] for p in range(n_pat)])


def _attn_prompt(slopes, q, k, v):
    b, _, seq, _ = q.shape
    assert all(seq // d <= ATT_BLOCK or w // d <= ATT_BLOCK for w, d in PATTERNS)
    blk = pl.BlockSpec((1, 1, seq, HEAD_DIM), lambda bi, h, *_: (bi, h, 0, 0))
    stats = pltpu.VMEM((len(PATTERNS), seq, HEAD_DIM), F32)
    return pl.pallas_call(
        _attn_prompt_kernel,
        grid_spec=pltpu.PrefetchScalarGridSpec(
            num_scalar_prefetch=1, grid=(b, N_ATT_HEADS),
            in_specs=[blk, blk, blk],
            out_specs=pl.BlockSpec((1, seq, HEAD_DIM), lambda bi, h, *_: (bi, 0, h)),
            scratch_shapes=[stats, stats, stats]),
        out_shape=jax.ShapeDtypeStruct((b, seq, D_ATT), F32),
        compiler_params=_params("arbitrary", "arbitrary"),
    )(slopes, q, k, v)


def _attn_sample_kernel(slopes_ref, qn_ref, kn_ref, vn_ref, ck_ref, cv_ref,
                        att_ref, ok_ref, ov_ref, *, heads_per_step):
    t_new = qn_ref.shape[2]
    n_past = ck_ref.shape[2]
    for c_ref, n_ref, o_ref in ((ck_ref, kn_ref, ok_ref), (cv_ref, vn_ref, ov_ref)):
        o_ref[0, :, 0:n_past - t_new, :] = c_ref[0, :, t_new:n_past, :]
        o_ref[0, :, n_past - t_new:n_past, :] = n_ref[0]
    t_row = lax.broadcasted_iota(jnp.int32, (t_new, n_past), 0)
    d_old = n_past + t_row - lax.broadcasted_iota(jnp.int32, (t_new, n_past), 1)
    d_new = (lax.broadcasted_iota(jnp.int32, (t_new, t_new), 0)
             - lax.broadcasted_iota(jnp.int32, (t_new, t_new), 1))
    for hh in range(heads_per_step):
        slope = slopes_ref[pl.program_id(1) * heads_per_step + hh]
        cols = slice(hh * HEAD_DIM, (hh + 1) * HEAD_DIM)
        q = qn_ref[0, hh]
        sb_old = _qk(q, ck_ref[0, hh]) - slope * d_old.astype(F32)
        sb_new = _qk(q, kn_ref[0, hh]) - slope * d_new.astype(F32)
        v_old = cv_ref[0, hh].astype(BF16)
        v_new = vn_ref[0, hh].astype(BF16)
        ms, ls, os_ = [], [], []
        for window, dil in PATTERNS:
            sp_old, m_old = _pattern_softmax(sb_old, d_old, window, dil)
            sp_new, m_new = _pattern_softmax(sb_new, d_new, window, dil)
            m = jnp.maximum(m_old, m_new)
            p_old = jnp.exp(sp_old - m)
            p_new = jnp.exp(sp_new - m)
            ms.append(m)
            ls.append(jnp.sum(p_old, axis=-1, keepdims=True) + jnp.sum(p_new, axis=-1, keepdims=True))
            os_.append(jnp.dot(p_old.astype(BF16), v_old, preferred_element_type=F32)
                       + jnp.dot(p_new.astype(BF16), v_new, preferred_element_type=F32))
        att_ref[0, :, cols] = _merge_patterns(ms, ls, os_)


def _attn_sample(slopes, qn, kn, vn, cache_k, cache_v):
    b, _, t_new, _ = qn.shape
    n_past = cache_k.shape[2]
    hps = 2
    new_blk = pl.BlockSpec((1, hps, t_new, HEAD_DIM), lambda bi, j, *_: (bi, j, 0, 0))
    buf_blk = pl.BlockSpec((1, hps, n_past, HEAD_DIM), lambda bi, j, *_: (bi, j, 0, 0))
    att_blk = pl.BlockSpec((1, t_new, hps * HEAD_DIM), lambda bi, j, *_: (bi, 0, j))
    return pl.pallas_call(
        functools.partial(_attn_sample_kernel, heads_per_step=hps),
        grid_spec=pltpu.PrefetchScalarGridSpec(
            num_scalar_prefetch=1, grid=(b, N_ATT_HEADS // hps),
            in_specs=[new_blk, new_blk, new_blk, buf_blk, buf_blk],
            out_specs=[att_blk, buf_blk, buf_blk]),
        out_shape=[jax.ShapeDtypeStruct((b, t_new, D_ATT), F32),
                   jax.ShapeDtypeStruct(cache_k.shape, F32),
                   jax.ShapeDtypeStruct(cache_v.shape, F32)],
        compiler_params=_params("arbitrary", "arbitrary"),
    )(slopes, qn, kn, vn, cache_k, cache_v)


def _recurrent_kernel(xr_ref, gate_ref, cs_ref, h0_ref, cw_ref, cb_ref, wa_ref, ba_ref,
                      wx_ref, bx_ref, lam_ref, rec_ref, nc_ref, hl_ref,
                      xext_ref, a_ref, u_ref, *, chunk):
    t_len = xr_ref.shape[1]
    pad = V7X_SUBLANES
    hist = CONV_WIDTH - 1
    xext_ref[pad - hist:pad, :] = cs_ref[0]
    xext_ref[pad:pad + t_len, :] = xr_ref[0]
    nc_ref[0] = xext_ref[pad + t_len - hist:pad + t_len, :]
    z = -lam_ref[...]
    softplus = jnp.maximum(z, 0.0) + jnp.log(1.0 + jnp.exp(-jnp.abs(z)))
    for c in range(t_len // chunk):
        r0 = c * chunk
        xc = cb_ref[...]
        for j in range(CONV_WIDTH):
            xc = xc + xext_ref[pad - hist + j + r0:pad - hist + j + r0 + chunk, :] * cw_ref[j:j + 1, :]
        xcb = xc.astype(BF16)
        r = jax.nn.sigmoid(jnp.dot(xcb, wa_ref[...], preferred_element_type=F32) + ba_ref[...])
        i = jax.nn.sigmoid(jnp.dot(xcb, wx_ref[...], preferred_element_type=F32) + bx_ref[...])
        log_a = -LRU_C * r * softplus
        a = jnp.exp(log_a)
        a_ref[r0:r0 + chunk, :] = a
        u_ref[r0:r0 + chunk, :] = jnp.sqrt(1.0 - jnp.exp(2.0 * log_a)) * (i * xc)

    rows = V7X_SUBLANES

    def step(tile, h):
        base = pl.multiple_of(tile * rows, rows)
        a = a_ref[pl.ds(base, rows), :]
        u = u_ref[pl.ds(base, rows), :]
        out = []
        for s in range(rows):
            h = a[s:s + 1, :] * h + u[s:s + 1, :]
            out.append(h)
        a_ref[pl.ds(base, rows), :] = jnp.concatenate(out, axis=0)
        return h

    h_last = lax.fori_loop(0, t_len // rows, step, h0_ref[0])
    hl_ref[0] = h_last
    for c in range(t_len // chunk):
        r0 = c * chunk
        rec_ref[0, r0:r0 + chunk, :] = a_ref[r0:r0 + chunk, :] * _gelu(gate_ref[0, r0:r0 + chunk, :])


def _recurrent(xr, gate, conv_state, h0, conv_w, conv_b, wa_bd, b_a, wx_bd, b_x, lam):
    b, t_len, c = xr.shape
    chunk = min(ROW_TILE, t_len)
    hist = CONV_WIDTH - 1
    seq_blk = pl.BlockSpec((1, t_len, c), lambda bi: (bi, 0, 0))
    row = lambda n: pl.BlockSpec((1, n, c), lambda bi: (bi, 0, 0))
    vec = lambda a: pl.BlockSpec(a.shape, lambda bi: (0,) * a.ndim)
    weights = (conv_w, conv_b, wa_bd, b_a, wx_bd, b_x, lam)
    return pl.pallas_call(
        functools.partial(_recurrent_kernel, chunk=chunk),
        grid=(b,),
        in_specs=[seq_blk, seq_blk, row(hist), row(1)] + [vec(w) for w in weights],
        out_specs=[seq_blk, row(hist), row(1)],
        out_shape=[jax.ShapeDtypeStruct((b, t_len, c), F32),
                   jax.ShapeDtypeStruct((b, hist, c), F32),
                   jax.ShapeDtypeStruct((b, 1, c), F32)],
        scratch_shapes=[pltpu.VMEM((V7X_SUBLANES + t_len, c), F32),
                        pltpu.VMEM((t_len, c), F32), pltpu.VMEM((t_len, c), F32)],
        compiler_params=_params("arbitrary"),
    )(xr, gate, conv_state, h0, *weights)


def _out_proj_kernel(att_ref, rec_ref, x_ref, w_ref, g_ref, b_ref, o_ref):
    d_att = att_ref.shape[-1]
    mix = (jnp.dot(att_ref[...].astype(BF16), w_ref[0:d_att, :], preferred_element_type=F32)
           + jnp.dot(rec_ref[...].astype(BF16), w_ref[d_att:, :], preferred_element_type=F32))
    o_ref[...] = _layer_norm(ALPHA * x_ref[...] + mix, g_ref[...], b_ref[...])


def _out_proj(att, rec, x2d, w_bf16, g, b):
    m, d_model = x2d.shape
    tm = min(ROW_TILE, m)
    rows = lambda a: pl.BlockSpec((tm, a.shape[1]), lambda i: (i, 0))
    return pl.pallas_call(
        _out_proj_kernel,
        grid=(m // tm,),
        in_specs=[rows(att), rows(rec), rows(x2d), _resident(w_bf16.shape),
                  _resident(g.shape), _resident(b.shape)],
        out_specs=rows(x2d),
        out_shape=jax.ShapeDtypeStruct((m, d_model), F32),
        compiler_params=_params("arbitrary"),
    )(att, rec, x2d, w_bf16, g, b)


def _top_k_rows(vals, k):
    n = vals.shape[0]
    row = lax.broadcasted_iota(jnp.int32, vals.shape, 0)
    tv, ti = [], []
    for _ in range(k):
        m = jnp.max(vals, axis=0, keepdims=True)
        idx = jnp.min(jnp.where(vals == m, row, n), axis=0, keepdims=True)
        tv.append(m)
        ti.append(idx)
        vals = jnp.where(row == idx, -jnp.inf, vals)
    return jnp.concatenate(tv, axis=0), jnp.concatenate(ti, axis=0)


def _route_kernel(x_ref, wq_ref, keys_ref, eidx_ref, gate_ref):
    q = jnp.dot(x_ref[...].astype(BF16), wq_ref[...], preferred_element_type=F32)
    tokens = q.shape[0]
    e_rows, g_rows = [], []
    for h in range(PEER_HEADS):
        sv, si = [], []
        for p in range(2):
            c0 = (h * 2 + p) * HALF_KEY
            s = lax.dot_general(keys_ref[h, p].astype(BF16), q[:, c0:c0 + HALF_KEY].astype(BF16),
                                (((1,), (1,)), ((), ())), preferred_element_type=F32)
            v, i = _top_k_rows(s, TOPK)
            sv.append(v)
            si.append(i)
        widths = [TOPK // (a + 1) for a in range(TOPK)]
        cand = jnp.concatenate([sv[0][a:a + 1] + sv[1][0:w] for a, w in enumerate(widths)], axis=0)
        cand_idx = jnp.concatenate([si[0][a:a + 1] * N_KEYS + si[1][0:w]
                                    for a, w in enumerate(widths)], axis=0)
        pad = -cand.shape[0] % V7X_SUBLANES
        cand = jnp.concatenate([cand, jnp.full((pad, tokens), -jnp.inf, F32)], axis=0)
        cand_idx = jnp.concatenate([cand_idx, jnp.zeros((pad, tokens), jnp.int32)], axis=0)
        fv, fi = _top_k_rows(cand, TOPK)
        flat = lax.broadcasted_iota(jnp.int32, cand.shape, 0)
        eidx = jnp.concatenate(
            [jnp.sum(jnp.where(flat == fi[r:r + 1, :], cand_idx, 0), axis=0, keepdims=True)
             for r in range(TOPK)], axis=0)
        ex = jnp.exp(fv - fv[0:1, :])
        e_rows.append(eidx)
        g_rows.append(ex / jnp.sum(ex, axis=0, keepdims=True))
    eidx_ref[...] = jnp.concatenate(e_rows, axis=0).T
    gates = jnp.concatenate(g_rows, axis=0)
    for j in range(tokens // GATHER_TOKENS):
        gate_ref[j] = gates[:, j * GATHER_TOKENS:(j + 1) * GATHER_TOKENS]


def _route(x1, wq_bf16, sub_keys):
    n, d_model = x1.shape
    tm = min(ROUTE_TILE, n)
    groups = tm // GATHER_TOKENS
    return pl.pallas_call(
        _route_kernel,
        grid=(n // tm,),
        in_specs=[pl.BlockSpec((tm, d_model), lambda i: (i, 0)), _resident(wq_bf16.shape),
                  _resident(sub_keys.shape)],
        out_specs=[pl.BlockSpec((tm, N_PICKS), lambda i: (i, 0)),
                   pl.BlockSpec((groups, N_PICKS, GATHER_TOKENS), lambda i: (i, 0, 0))],
        out_shape=[jax.ShapeDtypeStruct((n, N_PICKS), jnp.int32),
                   jax.ShapeDtypeStruct((n // GATHER_TOKENS, N_PICKS, GATHER_TOKENS), F32)],
        compiler_params=_params("arbitrary"),
    )(x1, wq_bf16, sub_keys)


def _pack_kernel(u_ref, v_ref, o_ref):
    half = u_ref.shape[1] // 2

    def words(ref):
        bits = lax.bitcast_convert_type(ref[...].astype(BF16).astype(F32), jnp.uint32)
        return (bits[:, :half] >> 16) | (bits[:, half:] & jnp.uint32(0xFFFF0000))

    o_ref[...] = jnp.concatenate([words(u_ref), words(v_ref)], axis=1)[:, None, :]


def _pack_experts(expert_u, expert_v):
    n, d = expert_u.shape
    rows = min(ROW_TILE, n)
    blk = pl.BlockSpec((rows, d), lambda i: (i, 0))
    return pl.pallas_call(
        _pack_kernel,
        grid=(n // rows,),
        in_specs=[blk, blk],
        out_specs=pl.BlockSpec((rows, 1, d), lambda i: (i, 0, 0)),
        out_shape=jax.ShapeDtypeStruct((n, 1, d), jnp.uint32),
        compiler_params=_params("arbitrary"),
    )(expert_u, expert_v)


def _unpack_words(words):
    low = lax.bitcast_convert_type(words << 16, F32)
    high = lax.bitcast_convert_type(words & jnp.uint32(0xFFFF0000), F32)
    return low, high


def _expert_kernel(idx_ref, x_ref, gate_ref, g_ref, b_ref, tab_hbm, o_ref, *scratch, n_blocks):
    step = pl.program_id(0)
    tokens, d_model = x_ref.shape
    half = d_model // 2
    *bufs, sems = scratch
    n_slots = len(bufs)

    lane_tiles = d_model // V7X_LANES

    def slot_wait(slot):
        copied = bufs[slot].at[pl.ds(0, tokens * N_PICKS * lane_tiles)]
        pltpu.make_async_copy(copied, copied, sems.at[slot]).wait()

    def pick_words(buf, t, tile0, n_tiles):
        base = t * N_PICKS * ROW_PITCH
        return jnp.concatenate(
            [buf[pl.ds(base + j, N_PICKS, stride=ROW_PITCH), :] for j in range(tile0, tile0 + n_tiles)],
            axis=1)

    @pl.when(step == 0)
    def _zero_stand_ins():
        for buf in bufs[1:]:
            buf[...] = jnp.zeros_like(buf)

    for slot in range(n_slots):
        pl.when((step >= GATHER_LAG) & ((step - GATHER_LAG) % n_slots == slot))(
            functools.partial(slot_wait, slot))

    def gather_and_finish(slot):
        for t in range(tokens):
            for r in range(N_PICKS):
                dst_row = (t * N_PICKS + r) * ROW_PITCH
                pltpu.make_async_copy(tab_hbm.at[idx_ref[t, r]],
                                      bufs[slot].at[pl.ds(dst_row, lane_tiles)],
                                      sems.at[slot]).start(priority=r % 2)
        rows = bufs[(slot - GATHER_LAG) % n_slots]
        x = x_ref[...]
        hid_cols = []
        for t in range(tokens):
            u_low, u_high = _unpack_words(pick_words(rows, t, 0, lane_tiles // 2))
            prod = u_low * x[t:t + 1, 0:half] + u_high * x[t:t + 1, half:d_model]
            hid_cols.append(jnp.sum(prod, axis=-1, keepdims=True))
        w = gate_ref[0] * _gelu(jnp.concatenate(hid_cols, axis=1))
        outs = []
        for t in range(tokens):
            v_low, v_high = _unpack_words(pick_words(rows, t, lane_tiles // 2, lane_tiles // 2))
            wt = w[:, t:t + 1]
            outs.append(jnp.concatenate([jnp.sum(wt * v_low, axis=0, keepdims=True),
                                         jnp.sum(wt * v_high, axis=0, keepdims=True)], axis=1))
        ffn = jnp.concatenate(outs, axis=0)
        o_ref[...] = _layer_norm(ALPHA * x + ffn, g_ref[...], b_ref[...])

    for slot in range(n_slots):
        pl.when(step % n_slots == slot)(functools.partial(gather_and_finish, slot))

    @pl.when(step == n_blocks + GATHER_LAG - 1)
    def _drain():
        for lag in range(GATHER_LAG):
            slot_wait((n_blocks + lag) % n_slots)


def _experts(eidx, gates, x1, g, b, table):
    n, d_model = x1.shape
    assert ROW_PITCH == d_model // V7X_LANES + 1
    tokens = GATHER_TOKENS
    n_blocks = n // tokens
    n_slots = GATHER_LAG + 1
    done = lambda i: jnp.maximum(i - GATHER_LAG, 0)
    tok_blk = lambda w: pl.BlockSpec((tokens, w), lambda i: (done(i), 0))
    return pl.pallas_call(
        functools.partial(_expert_kernel, n_blocks=n_blocks),
        grid=(n_blocks + GATHER_LAG,),
        in_specs=[pl.BlockSpec((tokens, N_PICKS), lambda i: (jnp.minimum(i, n_blocks - 1), 0),
                               memory_space=pltpu.SMEM),
                  tok_blk(d_model),
                  pl.BlockSpec((1, N_PICKS, tokens), lambda i: (done(i), 0, 0)),
                  _resident(g.shape), _resident(b.shape), pl.BlockSpec(memory_space=pl.ANY)],
        out_specs=tok_blk(d_model),
        out_shape=jax.ShapeDtypeStruct((n, d_model), F32),
        scratch_shapes=[pltpu.VMEM((tokens * N_PICKS * ROW_PITCH, V7X_LANES), jnp.uint32)] * n_slots
                       + [pltpu.SemaphoreType.DMA((n_slots,))],
        compiler_params=_params("arbitrary"),
    )(eidx, x1, gates, g, b, table.reshape(table.shape[0], d_model // V7X_LANES, V7X_LANES))


def _block_diag(w):
    n, c, d = w.shape
    eye = jnp.eye(n, dtype=w.dtype)
    return (eye[:, None, :, None] * w[:, :, None, :]).reshape(n * c, n * d)


def _prepare_weights(w_in, conv_w, conv_b, w_a, b_a, w_x, b_x, lam, w_out, ln1_g, ln1_b,
                     w_query, sub_keys, expert_u, expert_v, ln2_g, ln2_b):
    return (w_in.astype(BF16), conv_w, conv_b, _block_diag(w_a).astype(BF16), b_a,
            _block_diag(w_x).astype(BF16), b_x, lam, w_out.astype(BF16), ln1_g, ln1_b,
            w_query.astype(BF16), sub_keys, _pack_experts(expert_u, expert_v), ln2_g, ln2_b)


def _trunk_layer(x, cache, conv_state, h0, wts):
    (w_in, conv_w, conv_b, wa_bd, b_a, wx_bd, b_x, lam, w_out, ln1_g, ln1_b,
     w_query, sub_keys, expert_table, ln2_g, ln2_b) = wts
    b, t_len, d_model = x.shape
    d_lru = d_model - D_ATT
    x2d = x.reshape(b * t_len, d_model)
    slopes = 2.0 ** (-8.0 * jnp.arange(1, N_ATT_HEADS + 1, dtype=F32) / N_ATT_HEADS)

    head_major = lambda a: jnp.transpose(a, (0, 2, 1, 3))
    q, k, v, xr, gate = _in_proj(x2d, w_in, b, t_len, d_lru)
    seq = lambda a: a.reshape(b, t_len, a.shape[-1])
    if cache is None:
        att = _attn_prompt(slopes, q, k, v)
        keep = min(PATTERNS[-1][0], t_len)
        new_k, new_v = k[:, :, t_len - keep:], v[:, :, t_len - keep:]
    else:
        att, new_k, new_v = _attn_sample(slopes, q, k, v, head_major(cache[0]), head_major(cache[1]))
    row = lambda a: a.reshape(1, -1)
    rec, new_conv, h_last = _recurrent(
        seq(xr), seq(gate), conv_state, h0.reshape(b, 1, d_lru), conv_w, row(conv_b),
        wa_bd, row(b_a), wx_bd, row(b_x), row(lam))
    x1 = _out_proj(att.reshape(b * t_len, D_ATT), rec.reshape(b * t_len, d_lru), x2d,
                   w_out, row(ln1_g), row(ln1_b))
    eidx, gates2 = _route(x1, w_query, sub_keys)
    y = _experts(eidx, gates2, x1, row(ln2_g), row(ln2_b), expert_table)
    return (y.reshape(b, t_len, d_model), head_major(new_k), head_major(new_v), new_conv,
            h_last.reshape(b, d_lru))


def kernel(x_prompt, x_sample, cache_k, cache_v, state_conv, state_h, w_in, conv_w, conv_b, lru_w_a, lru_b_a, lru_w_x, lru_b_x, lru_lambda, w_out, ln1_g, ln1_b, peer_w_query, peer_sub_keys, peer_u, peer_v, ln2_g, ln2_b):
    yp, ys = x_prompt, x_sample
    outs_p, outs_s = [], []
    for layer in range(w_in.shape[0]):
        wts = _prepare_weights(*(w[layer] for w in (
            w_in, conv_w, conv_b, lru_w_a, lru_b_a, lru_w_x, lru_b_x, lru_lambda, w_out, ln1_g,
            ln1_b, peer_w_query, peer_sub_keys, peer_u, peer_v, ln2_g, ln2_b)))
        bp = yp.shape[0]
        d_lru = conv_w.shape[-1]
        yp, *rest_p = _trunk_layer(yp, None, jnp.zeros((bp, CONV_WIDTH - 1, d_lru), F32),
                                   jnp.zeros((bp, d_lru), F32), wts)
        ys, *rest_s = _trunk_layer(ys, (cache_k[layer], cache_v[layer]), state_conv[layer],
                                   state_h[layer], wts)
        outs_p.append(rest_p)
        outs_s.append(rest_s)
    stack = lambda outs, j: jnp.stack([o[j] for o in outs])
    return (yp, ys, *(stack(outs_p, j) for j in range(4)), *(stack(outs_s, j) for j in range(4)))
```

```python
import functools

import jax
import jax.numpy as jnp
from jax import lax
from jax.experimental import pallas as pl
from jax.experimental.pallas import tpu as pltpu

F32 = jnp.float32
BF16 = jnp.bfloat16

N_ATT_HEADS = 12
HEAD_DIM = 128
D_ATT = N_ATT_HEADS * HEAD_DIM
ATT_SCALE = HEAD_DIM ** -0.5
PATTERNS = ((128, 1), (512, 4), (2048, 16))
N_LRU_BLOCKS = 8
CONV_WIDTH = 4
LRU_C = 8.0
N_KEYS = 128
PEER_HEADS = 8
HALF_KEY = 128
TOPK = 16
DEPTH = 1
ALPHA = (2.0 * DEPTH) ** 0.25
LN_EPS = 1e-5

V7X_SUBLANES = 8
V7X_LANES = 128
V7X_VMEM_LIMIT_BYTES = 56 * 1024 * 1024

ROW_TILE = 256
ATT_BLOCK = 128
ROUTE_TILE = 128
GATHER_TOKENS = 8
ROW_PITCH = 2048 // V7X_LANES + 1
GATHER_LAG = 2
N_PICKS = PEER_HEADS * TOPK


def _params(*sem):
    return pltpu.CompilerParams(dimension_semantics=sem, vmem_limit_bytes=V7X_VMEM_LIMIT_BYTES)


def _resident(shape):
    return pl.BlockSpec(shape, lambda *_: (0,) * len(shape), pipeline_mode=pl.Buffered(1))


def _layer_norm(z, g, b):
    mu = jnp.mean(z, axis=-1, keepdims=True)
    zc = z - mu
    var = jnp.mean(zc * zc, axis=-1, keepdims=True)
    return zc * lax.rsqrt(var + LN_EPS) * g + b


def _gelu(x):
    return 0.5 * x * (1.0 + lax.erf(x * (2.0 ** -0.5)))


def _in_proj_kernel(x_ref, w_ref, q_ref, k_ref, v_ref, xr_ref, gate_ref):
    xb = x_ref[...].astype(BF16)
    col = 0
    for o_ref in (q_ref, k_ref, v_ref):
        res = jnp.dot(xb, w_ref[:, col:col + D_ATT], preferred_element_type=F32)
        nb, _, tr, _ = o_ref.shape
        for bb in range(nb):
            for h in range(N_ATT_HEADS):
                o_ref[bb, h] = res[bb * tr:(bb + 1) * tr, h * HEAD_DIM:(h + 1) * HEAD_DIM]
        col += D_ATT
    for o_ref in (xr_ref, gate_ref):
        width = o_ref.shape[-1]
        o_ref[...] = jnp.dot(xb, w_ref[:, col:col + width], preferred_element_type=F32)
        col += width


def _in_proj(x2d, w_bf16, batch, t_len, d_lru):
    m, d_model = x2d.shape
    tm = min(ROW_TILE, m)
    tr = min(tm, t_len)
    nb = tm // tr
    tiles_per_batch = t_len // tr
    head_blk = pl.BlockSpec((nb, N_ATT_HEADS, tr, HEAD_DIM),
                            lambda i: (i // tiles_per_batch, 0, i % tiles_per_batch, 0))
    head_shape = jax.ShapeDtypeStruct((batch, N_ATT_HEADS, t_len, HEAD_DIM), F32)
    return pl.pallas_call(
        _in_proj_kernel,
        grid=(m // tm,),
        in_specs=[pl.BlockSpec((tm, d_model), lambda i: (i, 0)), _resident(w_bf16.shape)],
        out_specs=[head_blk] * 3 + [pl.BlockSpec((tm, d_lru), lambda i: (i, 0))] * 2,
        out_shape=[head_shape] * 3 + [jax.ShapeDtypeStruct((m, d_lru), F32)] * 2,
        compiler_params=_params("arbitrary"),
    )(x2d, w_bf16)


def _pattern_softmax(sb, delta, window, dil):
    valid = (delta >= 0) & (delta <= window) & ((delta & (dil - 1)) == 0)
    sp = jnp.where(valid, sb, -jnp.inf)
    m = jnp.max(sp, axis=-1, keepdims=True)
    return sp, m


def _merge_patterns(ms, ls, os_):
    m_max = functools.reduce(jnp.maximum, ms)
    ws = [jnp.exp(m - m_max) for m in ms]
    num = sum(w * o for w, o in zip(ws, os_))
    den = sum(w * l for w, l in zip(ws, ls))
    return num / den


def _qk(q, k):
    return lax.dot_general(q.astype(BF16), k.astype(BF16), (((1,), (1,)), ((), ())),
                           preferred_element_type=F32) * ATT_SCALE


def _attn_prompt_kernel(slopes_ref, q_ref, k_ref, v_ref, o_ref, m_scr, l_scr, acc_scr):
    seq = q_ref.shape[2]
    slope = slopes_ref[pl.program_id(1)]
    for p, (window, dil) in enumerate(PATTERNS):
        n_class = seq // dil
        reach = window // dil
        blk = min(ATT_BLOCK, n_class)
        for r in range(dil):
            def class_rows(ref, start, size):
                return ref[0, 0, pl.ds(r + dil * start, size, stride=dil), :] if dil > 1 else \
                    ref[0, 0, start:start + size, :]
            for c in range(n_class // blk):
                k0 = max(0, c * blk - reach)
                n_keys = (c + 1) * blk - k0
                s = _qk(class_rows(q_ref, c * blk, blk), class_rows(k_ref, k0, n_keys))
                dist = (c * blk - k0 + lax.broadcasted_iota(jnp.int32, (blk, n_keys), 0)
                        - lax.broadcasted_iota(jnp.int32, (blk, n_keys), 1))
                sp = jnp.where((dist >= 0) & (dist <= reach),
                               s - (slope * dil) * dist.astype(F32), -jnp.inf)
                m = jnp.max(sp, axis=-1, keepdims=True)
                prob = jnp.exp(sp - m)
                out = jnp.dot(prob.astype(BF16), class_rows(v_ref, k0, n_keys).astype(BF16),
                              preferred_element_type=F32)
                dst = pl.ds(r + dil * c * blk, blk, stride=dil) if dil > 1 else \
                    pl.ds(c * blk, blk)
                m_scr[p, dst, :] = jnp.broadcast_to(m, out.shape)
                l_scr[p, dst, :] = jnp.broadcast_to(jnp.sum(prob, axis=-1, keepdims=True), out.shape)
                acc_scr[p, dst, :] = out
    n_pat = len(PATTERNS)
    o_ref[0] = _merge_patterns([m_scr[p] for p in range(n_pat)], [l_scr[p] for p in range(n_pat)],
                               [acc_scr[p] for p in range(n_pat)])


def _attn_prompt(slopes, q, k, v):
    b, _, seq, _ = q.shape
    assert all(seq // d <= ATT_BLOCK or w // d <= ATT_BLOCK for w, d in PATTERNS)
    blk = pl.BlockSpec((1, 1, seq, HEAD_DIM), lambda bi, h, *_: (bi, h, 0, 0))
    stats = pltpu.VMEM((len(PATTERNS), seq, HEAD_DIM), F32)
    return pl.pallas_call(
        _attn_prompt_kernel,
        grid_spec=pltpu.PrefetchScalarGridSpec(
            num_scalar_prefetch=1, grid=(b, N_ATT_HEADS),
            in_specs=[blk, blk, blk],
            out_specs=pl.BlockSpec((1, seq, HEAD_DIM), lambda bi, h, *_: (bi, 0, h)),
            scratch_shapes=[stats, stats, stats]),
        out_shape=jax.ShapeDtypeStruct((b, seq, D_ATT), F32),
        compiler_params=_params("arbitrary", "arbitrary"),
    )(slopes, q, k, v)


def _attn_sample_kernel(slopes_ref, qn_ref, kn_ref, vn_ref, ck_ref, cv_ref,
                        att_ref, ok_ref, ov_ref, *, heads_per_step):
    t_new = qn_ref.shape[2]
    n_past = ck_ref.shape[2]
    for c_ref, n_ref, o_ref in ((ck_ref, kn_ref, ok_ref), (cv_ref, vn_ref, ov_ref)):
        o_ref[0, :, 0:n_past - t_new, :] = c_ref[0, :, t_new:n_past, :]
        o_ref[0, :, n_past - t_new:n_past, :] = n_ref[0]
    t_row = lax.broadcasted_iota(jnp.int32, (t_new, n_past), 0)
    d_old = n_past + t_row - lax.broadcasted_iota(jnp.int32, (t_new, n_past), 1)
    d_new = (lax.broadcasted_iota(jnp.int32, (t_new, t_new), 0)
             - lax.broadcasted_iota(jnp.int32, (t_new, t_new), 1))
    for hh in range(heads_per_step):
        slope = slopes_ref[pl.program_id(1) * heads_per_step + hh]
        cols = slice(hh * HEAD_DIM, (hh + 1) * HEAD_DIM)
        q = qn_ref[0, hh]
        sb_old = _qk(q, ck_ref[0, hh]) - slope * d_old.astype(F32)
        sb_new = _qk(q, kn_ref[0, hh]) - slope * d_new.astype(F32)
        v_old = cv_ref[0, hh].astype(BF16)
        v_new = vn_ref[0, hh].astype(BF16)
        ms, ls, os_ = [], [], []
        for window, dil in PATTERNS:
            sp_old, m_old = _pattern_softmax(sb_old, d_old, window, dil)
            sp_new, m_new = _pattern_softmax(sb_new, d_new, window, dil)
            m = jnp.maximum(m_old, m_new)
            p_old = jnp.exp(sp_old - m)
            p_new = jnp.exp(sp_new - m)
            ms.append(m)
            ls.append(jnp.sum(p_old, axis=-1, keepdims=True) + jnp.sum(p_new, axis=-1, keepdims=True))
            os_.append(jnp.dot(p_old.astype(BF16), v_old, preferred_element_type=F32)
                       + jnp.dot(p_new.astype(BF16), v_new, preferred_element_type=F32))
        att_ref[0, :, cols] = _merge_patterns(ms, ls, os_)


def _attn_sample(slopes, qn, kn, vn, cache_k, cache_v):
    b, _, t_new, _ = qn.shape
    n_past = cache_k.shape[2]
    hps = 2
    new_blk = pl.BlockSpec((1, hps, t_new, HEAD_DIM), lambda bi, j, *_: (bi, j, 0, 0))
    buf_blk = pl.BlockSpec((1, hps, n_past, HEAD_DIM), lambda bi, j, *_: (bi, j, 0, 0))
    att_blk = pl.BlockSpec((1, t_new, hps * HEAD_DIM), lambda bi, j, *_: (bi, 0, j))
    return pl.pallas_call(
        functools.partial(_attn_sample_kernel, heads_per_step=hps),
        grid_spec=pltpu.PrefetchScalarGridSpec(
            num_scalar_prefetch=1, grid=(b, N_ATT_HEADS // hps),
            in_specs=[new_blk, new_blk, new_blk, buf_blk, buf_blk],
            out_specs=[att_blk, buf_blk, buf_blk]),
        out_shape=[jax.ShapeDtypeStruct((b, t_new, D_ATT), F32),
                   jax.ShapeDtypeStruct(cache_k.shape, F32),
                   jax.ShapeDtypeStruct(cache_v.shape, F32)],
        compiler_params=_params("arbitrary", "arbitrary"),
    )(slopes, qn, kn, vn, cache_k, cache_v)


def _recurrent_kernel(xr_ref, gate_ref, cs_ref, h0_ref, cw_ref, cb_ref, wa_ref, ba_ref,
                      wx_ref, bx_ref, lam_ref, rec_ref, nc_ref, hl_ref,
                      xext_ref, a_ref, u_ref, *, chunk):
    t_len = xr_ref.shape[1]
    pad = V7X_SUBLANES
    hist = CONV_WIDTH - 1
    xext_ref[pad - hist:pad, :] = cs_ref[0]
    xext_ref[pad:pad + t_len, :] = xr_ref[0]
    nc_ref[0] = xext_ref[pad + t_len - hist:pad + t_len, :]
    z = -lam_ref[...]
    softplus = jnp.maximum(z, 0.0) + jnp.log(1.0 + jnp.exp(-jnp.abs(z)))
    for c in range(t_len // chunk):
        r0 = c * chunk
        xc = cb_ref[...]
        for j in range(CONV_WIDTH):
            xc = xc + xext_ref[pad - hist + j + r0:pad - hist + j + r0 + chunk, :] * cw_ref[j:j + 1, :]
        xcb = xc.astype(BF16)
        r = jax.nn.sigmoid(jnp.dot(xcb, wa_ref[...], preferred_element_type=F32) + ba_ref[...])
        i = jax.nn.sigmoid(jnp.dot(xcb, wx_ref[...], preferred_element_type=F32) + bx_ref[...])
        log_a = -LRU_C * r * softplus
        a = jnp.exp(log_a)
        a_ref[r0:r0 + chunk, :] = a
        u_ref[r0:r0 + chunk, :] = jnp.sqrt(1.0 - jnp.exp(2.0 * log_a)) * (i * xc)

    rows = V7X_SUBLANES

    def step(tile, h):
        base = pl.multiple_of(tile * rows, rows)
        a = a_ref[pl.ds(base, rows), :]
        u = u_ref[pl.ds(base, rows), :]
        out = []
        for s in range(rows):
            h = a[s:s + 1, :] * h + u[s:s + 1, :]
            out.append(h)
        a_ref[pl.ds(base, rows), :] = jnp.concatenate(out, axis=0)
        return h

    h_last = lax.fori_loop(0, t_len // rows, step, h0_ref[0])
    hl_ref[0] = h_last
    for c in range(t_len // chunk):
        r0 = c * chunk
        rec_ref[0, r0:r0 + chunk, :] = a_ref[r0:r0 + chunk, :] * _gelu(gate_ref[0, r0:r0 + chunk, :])


def _recurrent(xr, gate, conv_state, h0, conv_w, conv_b, wa_bd, b_a, wx_bd, b_x, lam):
    b, t_len, c = xr.shape
    chunk = min(ROW_TILE, t_len)
    hist = CONV_WIDTH - 1
    seq_blk = pl.BlockSpec((1, t_len, c), lambda bi: (bi, 0, 0))
    row = lambda n: pl.BlockSpec((1, n, c), lambda bi: (bi, 0, 0))
    vec = lambda a: pl.BlockSpec(a.shape, lambda bi: (0,) * a.ndim)
    weights = (conv_w, conv_b, wa_bd, b_a, wx_bd, b_x, lam)
    return pl.pallas_call(
        functools.partial(_recurrent_kernel, chunk=chunk),
        grid=(b,),
        in_specs=[seq_blk, seq_blk, row(hist), row(1)] + [vec(w) for w in weights],
        out_specs=[seq_blk, row(hist), row(1)],
        out_shape=[jax.ShapeDtypeStruct((b, t_len, c), F32),
                   jax.ShapeDtypeStruct((b, hist, c), F32),
                   jax.ShapeDtypeStruct((b, 1, c), F32)],
        scratch_shapes=[pltpu.VMEM((V7X_SUBLANES + t_len, c), F32),
                        pltpu.VMEM((t_len, c), F32), pltpu.VMEM((t_len, c), F32)],
        compiler_params=_params("arbitrary"),
    )(xr, gate, conv_state, h0, *weights)


def _out_proj_kernel(att_ref, rec_ref, x_ref, w_ref, g_ref, b_ref, o_ref):
    d_att = att_ref.shape[-1]
    mix = (jnp.dot(att_ref[...].astype(BF16), w_ref[0:d_att, :], preferred_element_type=F32)
           + jnp.dot(rec_ref[...].astype(BF16), w_ref[d_att:, :], preferred_element_type=F32))
    o_ref[...] = _layer_norm(ALPHA * x_ref[...] + mix, g_ref[...], b_ref[...])


def _out_proj(att, rec, x2d, w_bf16, g, b):
    m, d_model = x2d.shape
    tm = min(ROW_TILE, m)
    rows = lambda a: pl.BlockSpec((tm, a.shape[1]), lambda i: (i, 0))
    return pl.pallas_call(
        _out_proj_kernel,
        grid=(m // tm,),
        in_specs=[rows(att), rows(rec), rows(x2d), _resident(w_bf16.shape),
                  _resident(g.shape), _resident(b.shape)],
        out_specs=rows(x2d),
        out_shape=jax.ShapeDtypeStruct((m, d_model), F32),
        compiler_params=_params("arbitrary"),
    )(att, rec, x2d, w_bf16, g, b)


def _top_k_rows(vals, k):
    n = vals.shape[0]
    row = lax.broadcasted_iota(jnp.int32, vals.shape, 0)
    tv, ti = [], []
    for _ in range(k):
        m = jnp.max(vals, axis=0, keepdims=True)
        idx = jnp.min(jnp.where(vals == m, row, n), axis=0, keepdims=True)
        tv.append(m)
        ti.append(idx)
        vals = jnp.where(row == idx, -jnp.inf, vals)
    return jnp.concatenate(tv, axis=0), jnp.concatenate(ti, axis=0)


def _route_kernel(x_ref, wq_ref, keys_ref, eidx_ref, gate_ref):
    q = jnp.dot(x_ref[...].astype(BF16), wq_ref[...], preferred_element_type=F32)
    tokens = q.shape[0]
    e_rows, g_rows = [], []
    for h in range(PEER_HEADS):
        sv, si = [], []
        for p in range(2):
            c0 = (h * 2 + p) * HALF_KEY
            s = lax.dot_general(keys_ref[h, p].astype(BF16), q[:, c0:c0 + HALF_KEY].astype(BF16),
                                (((1,), (1,)), ((), ())), preferred_element_type=F32)
            v, i = _top_k_rows(s, TOPK)
            sv.append(v)
            si.append(i)
        widths = [TOPK // (a + 1) for a in range(TOPK)]
        cand = jnp.concatenate([sv[0][a:a + 1] + sv[1][0:w] for a, w in enumerate(widths)], axis=0)
        cand_idx = jnp.concatenate([si[0][a:a + 1] * N_KEYS + si[1][0:w]
                                    for a, w in enumerate(widths)], axis=0)
        pad = -cand.shape[0] % V7X_SUBLANES
        cand = jnp.concatenate([cand, jnp.full((pad, tokens), -jnp.inf, F32)], axis=0)
        cand_idx = jnp.concatenate([cand_idx, jnp.zeros((pad, tokens), jnp.int32)], axis=0)
        fv, fi = _top_k_rows(cand, TOPK)
        flat = lax.broadcasted_iota(jnp.int32, cand.shape, 0)
        eidx = jnp.concatenate(
            [jnp.sum(jnp.where(flat == fi[r:r + 1, :], cand_idx, 0), axis=0, keepdims=True)
             for r in range(TOPK)], axis=0)
        ex = jnp.exp(fv - fv[0:1, :])
        e_rows.append(eidx)
        g_rows.append(ex / jnp.sum(ex, axis=0, keepdims=True))
    eidx_ref[...] = jnp.concatenate(e_rows, axis=0).T
    gates = jnp.concatenate(g_rows, axis=0)
    for j in range(tokens // GATHER_TOKENS):
        gate_ref[j] = gates[:, j * GATHER_TOKENS:(j + 1) * GATHER_TOKENS]


def _route(x1, wq_bf16, sub_keys):
    n, d_model = x1.shape
    tm = min(ROUTE_TILE, n)
    groups = tm // GATHER_TOKENS
    return pl.pallas_call(
        _route_kernel,
        grid=(n // tm,),
        in_specs=[pl.BlockSpec((tm, d_model), lambda i: (i, 0)), _resident(wq_bf16.shape),
                  _resident(sub_keys.shape)],
        out_specs=[pl.BlockSpec((tm, N_PICKS), lambda i: (i, 0)),
                   pl.BlockSpec((groups, N_PICKS, GATHER_TOKENS), lambda i: (i, 0, 0))],
        out_shape=[jax.ShapeDtypeStruct((n, N_PICKS), jnp.int32),
                   jax.ShapeDtypeStruct((n // GATHER_TOKENS, N_PICKS, GATHER_TOKENS), F32)],
        compiler_params=_params("arbitrary"),
    )(x1, wq_bf16, sub_keys)


def _pack_kernel(u_ref, v_ref, o_ref):
    half = u_ref.shape[1] // 2

    def words(ref):
        bits = lax.bitcast_convert_type(ref[...].astype(BF16).astype(F32), jnp.uint32)
        return (bits[:, :half] >> 16) | (bits[:, half:] & jnp.uint32(0xFFFF0000))

    o_ref[...] = jnp.concatenate([words(u_ref), words(v_ref)], axis=1)[:, None, :]


def _pack_experts(expert_u, expert_v):
    n, d = expert_u.shape
    rows = min(ROW_TILE, n)
    blk = pl.BlockSpec((rows, d), lambda i: (i, 0))
    return pl.pallas_call(
        _pack_kernel,
        grid=(n // rows,),
        in_specs=[blk, blk],
        out_specs=pl.BlockSpec((rows, 1, d), lambda i: (i, 0, 0)),
        out_shape=jax.ShapeDtypeStruct((n, 1, d), jnp.uint32),
        compiler_params=_params("arbitrary"),
    )(expert_u, expert_v)


def _unpack_words(words):
    low = lax.bitcast_convert_type(words << 16, F32)
    high = lax.bitcast_convert_type(words & jnp.uint32(0xFFFF0000), F32)
    return low, high


def _expert_kernel(idx_ref, x_ref, gate_ref, g_ref, b_ref, tab_hbm, o_ref, *scratch, n_blocks):
    step = pl.program_id(0)
    tokens, d_model = x_ref.shape
    half = d_model // 2
    *bufs, sems = scratch
    n_slots = len(bufs)

    lane_tiles = d_model // V7X_LANES

    def slot_wait(slot):
        copied = bufs[slot].at[pl.ds(0, tokens * N_PICKS * lane_tiles)]
        pltpu.make_async_copy(copied, copied, sems.at[slot]).wait()

    def pick_words(buf, t, tile0, n_tiles):
        base = t * N_PICKS * ROW_PITCH
        return jnp.concatenate(
            [buf[pl.ds(base + j, N_PICKS, stride=ROW_PITCH), :] for j in range(tile0, tile0 + n_tiles)],
            axis=1)

    @pl.when(step == 0)
    def _zero_stand_ins():
        for buf in bufs[1:]:
            buf[...] = jnp.zeros_like(buf)

    for slot in range(n_slots):
        pl.when((step >= GATHER_LAG) & ((step - GATHER_LAG) % n_slots == slot))(
            functools.partial(slot_wait, slot))

    def gather_and_finish(slot):
        for t in range(tokens):
            for r in range(N_PICKS):
                dst_row = (t * N_PICKS + r) * ROW_PITCH
                pltpu.make_async_copy(tab_hbm.at[idx_ref[t, r]],
                                      bufs[slot].at[pl.ds(dst_row, lane_tiles)],
                                      sems.at[slot]).start(priority=r % 2)
        rows = bufs[(slot - GATHER_LAG) % n_slots]
        x = x_ref[...]
        n_half = lane_tiles // 2
        hid_cols = []
        for t in range(tokens):
            acc = None
            for j in range(n_half):
                u_low, u_high = _unpack_words(pick_words(rows, t, j, 1))
                lo = slice(j * V7X_LANES, (j + 1) * V7X_LANES)
                hi = slice(half + j * V7X_LANES, half + (j + 1) * V7X_LANES)
                term = u_low * x[t:t + 1, lo] + u_high * x[t:t + 1, hi]
                acc = term if acc is None else acc + term
            hid_cols.append(jnp.sum(acc, axis=-1, keepdims=True))
        w = gate_ref[0] * _gelu(jnp.concatenate(hid_cols, axis=1))
        outs = []
        for t in range(tokens):
            wt = w[:, t:t + 1]
            lows, highs = [], []
            for j in range(n_half):
                v_low, v_high = _unpack_words(pick_words(rows, t, n_half + j, 1))
                lows.append(jnp.sum(wt * v_low, axis=0, keepdims=True))
                highs.append(jnp.sum(wt * v_high, axis=0, keepdims=True))
            outs.append(jnp.concatenate(lows + highs, axis=1))
        ffn = jnp.concatenate(outs, axis=0)
        o_ref[...] = _layer_norm(ALPHA * x + ffn, g_ref[...], b_ref[...])

    for slot in range(n_slots):
        pl.when(step % n_slots == slot)(functools.partial(gather_and_finish, slot))

    @pl.when(step == n_blocks + GATHER_LAG - 1)
    def _drain():
        for lag in range(GATHER_LAG):
            slot_wait((n_blocks + lag) % n_slots)


def _experts(eidx, gates, x1, g, b, table):
    n, d_model = x1.shape
    assert ROW_PITCH == d_model // V7X_LANES + 1
    tokens = GATHER_TOKENS
    n_blocks = n // tokens
    n_slots = GATHER_LAG + 1
    done = lambda i: jnp.maximum(i - GATHER_LAG, 0)
    tok_blk = lambda w: pl.BlockSpec((tokens, w), lambda i: (done(i), 0))
    return pl.pallas_call(
        functools.partial(_expert_kernel, n_blocks=n_blocks),
        grid=(n_blocks + GATHER_LAG,),
        in_specs=[pl.BlockSpec((tokens, N_PICKS), lambda i: (jnp.minimum(i, n_blocks - 1), 0),
                               memory_space=pltpu.SMEM),
                  tok_blk(d_model),
                  pl.BlockSpec((1, N_PICKS, tokens), lambda i: (done(i), 0, 0)),
                  _resident(g.shape), _resident(b.shape), pl.BlockSpec(memory_space=pl.ANY)],
        out_specs=tok_blk(d_model),
        out_shape=jax.ShapeDtypeStruct((n, d_model), F32),
        scratch_shapes=[pltpu.VMEM((tokens * N_PICKS * ROW_PITCH, V7X_LANES), jnp.uint32)] * n_slots
                       + [pltpu.SemaphoreType.DMA((n_slots,))],
        compiler_params=_params("arbitrary"),
    )(eidx, x1, gates, g, b, table.reshape(table.shape[0], d_model // V7X_LANES, V7X_LANES))


def _block_diag(w):
    n, c, d = w.shape
    eye = jnp.eye(n, dtype=w.dtype)
    return (eye[:, None, :, None] * w[:, :, None, :]).reshape(n * c, n * d)


def _prepare_weights(w_in, conv_w, conv_b, w_a, b_a, w_x, b_x, lam, w_out, ln1_g, ln1_b,
                     w_query, sub_keys, expert_u, expert_v, ln2_g, ln2_b):
    return (w_in.astype(BF16), conv_w, conv_b, _block_diag(w_a).astype(BF16), b_a,
            _block_diag(w_x).astype(BF16), b_x, lam, w_out.astype(BF16), ln1_g, ln1_b,
            w_query.astype(BF16), sub_keys, _pack_experts(expert_u, expert_v), ln2_g, ln2_b)


def _trunk_layer(x, cache, conv_state, h0, wts):
    (w_in, conv_w, conv_b, wa_bd, b_a, wx_bd, b_x, lam, w_out, ln1_g, ln1_b,
     w_query, sub_keys, expert_table, ln2_g, ln2_b) = wts
    b, t_len, d_model = x.shape
    d_lru = d_model - D_ATT
    x2d = x.reshape(b * t_len, d_model)
    slopes = 2.0 ** (-8.0 * jnp.arange(1, N_ATT_HEADS + 1, dtype=F32) / N_ATT_HEADS)

    head_major = lambda a: jnp.transpose(a, (0, 2, 1, 3))
    q, k, v, xr, gate = _in_proj(x2d, w_in, b, t_len, d_lru)
    seq = lambda a: a.reshape(b, t_len, a.shape[-1])
    if cache is None:
        att = _attn_prompt(slopes, q, k, v)
        keep = min(PATTERNS[-1][0], t_len)
        new_k, new_v = k[:, :, t_len - keep:], v[:, :, t_len - keep:]
    else:
        att, new_k, new_v = _attn_sample(slopes, q, k, v, head_major(cache[0]), head_major(cache[1]))
    row = lambda a: a.reshape(1, -1)
    rec, new_conv, h_last = _recurrent(
        seq(xr), seq(gate), conv_state, h0.reshape(b, 1, d_lru), conv_w, row(conv_b),
        wa_bd, row(b_a), wx_bd, row(b_x), row(lam))
    x1 = _out_proj(att.reshape(b * t_len, D_ATT), rec.reshape(b * t_len, d_lru), x2d,
                   w_out, row(ln1_g), row(ln1_b))
    eidx, gates2 = _route(x1, w_query, sub_keys)
    y = _experts(eidx, gates2, x1, row(ln2_g), row(ln2_b), expert_table)
    return (y.reshape(b, t_len, d_model), head_major(new_k), head_major(new_v), new_conv,
            h_last.reshape(b, d_lru))


def kernel(x_prompt, x_sample, cache_k, cache_v, state_conv, state_h, w_in, conv_w, conv_b, lru_w_a, lru_b_a, lru_w_x, lru_b_x, lru_lambda, w_out, ln1_g, ln1_b, peer_w_query, peer_sub_keys, peer_u, peer_v, ln2_g, ln2_b):
    yp, ys = x_prompt, x_sample
    outs_p, outs_s = [], []
    for layer in range(w_in.shape[0]):
        wts = _prepare_weights(*(w[layer] for w in (
            w_in, conv_w, conv_b, lru_w_a, lru_b_a, lru_w_x, lru_b_x, lru_lambda, w_out, ln1_g,
            ln1_b, peer_w_query, peer_sub_keys, peer_u, peer_v, ln2_g, ln2_b)))
        bp = yp.shape[0]
        d_lru = conv_w.shape[-1]
        yp, *rest_p = _trunk_layer(yp, None, jnp.zeros((bp, CONV_WIDTH - 1, d_lru), F32),
                                   jnp.zeros((bp, d_lru), F32), wts)
        ys, *rest_s = _trunk_layer(ys, (cache_k[layer], cache_v[layer]), state_conv[layer],
                                   state_h[layer], wts)
        outs_p.append(rest_p)
        outs_s.append(rest_s)
    stack = lambda outs, j: jnp.stack([o[j] for o in outs])
    return (yp, ys, *(stack(outs_p, j) for j in range(4)), *(stack(outs_s, j) for j in range(4)))
```
